```python
import math
import jax, jax.numpy as jnp
from jax import lax
import numpy as np

D_MODEL = 2048
BATCH = 2
SEQ = 8192
DEPTH = 1

MEM_LEN = 256
DIFF_HEADS = 8
DIFF_QK_DIM = 64
DIFF_V_DIM = 2 * DIFF_QK_DIM
DIFF_QK_WIDTH = DIFF_HEADS * 2 * DIFF_QK_DIM
DIFF_WIDTH = DIFF_HEADS * DIFF_V_DIM
SSM_GROUPS = 32
SSM_GROUP_CH = 16
SSM_STATE = 64
SSM_WIDTH = SSM_GROUPS * SSM_GROUP_CH
XATTN_HEADS = 4
XATTN_HEAD_DIM = 128
XATTN_WIDTH = XATTN_HEADS * XATTN_HEAD_DIM

MIX_WIDTH = DIFF_WIDTH + SSM_WIDTH + XATTN_WIDTH
IN_COLS = 2 * DIFF_QK_WIDTH + DIFF_WIDTH + SSM_WIDTH + XATTN_WIDTH
D_FF = 4 * D_MODEL
ROPE_THETA = 10000.0
Q_BLOCK = 128
EPS = 1e-6
DT_MIN = 1e-3
DT_MAX = 1e-1

kernel_name = "hymba_diffattn_s5_memxattn_block"


def rms_norm(t, g):
    tf = t.astype(jnp.float32)
    y = tf * lax.rsqrt(jnp.mean(tf * tf, axis=-1, keepdims=True) + EPS)
    return (y * g.astype(jnp.float32)).astype(t.dtype)


def rotary_tables(seq_len, dim):
    inv_freq = 1.0 / (ROPE_THETA ** (jnp.arange(0, dim, 2, dtype=jnp.float32) / dim))
    pos = jnp.arange(seq_len, dtype=jnp.float32)
    freqs = pos[:, None] * inv_freq[None, :]
    emb = jnp.concatenate([freqs, freqs], axis=-1)
    return jnp.cos(emb), jnp.sin(emb)


def apply_rotary(t, cos, sin):
    tf = t.astype(jnp.float32)
    half = tf.shape[-1] // 2
    rot = jnp.concatenate([-tf[..., half:], tf[..., :half]], axis=-1)
    c = cos[None, :, None, None, :]
    s = sin[None, :, None, None, :]
    return (tf * c + rot * s).astype(t.dtype)


def diff_attention(q, k, v, lam):
    b, s = q.shape[0], q.shape[1]
    nblk = s // Q_BLOCK
    qf = q.astype(jnp.float32) * (DIFF_QK_DIM ** -0.5)
    qb = qf.reshape(b, nblk, Q_BLOCK, DIFF_HEADS, 2, DIFF_QK_DIM).transpose(1, 0, 3, 4, 2, 5)
    kt = k.astype(jnp.float32).transpose(0, 2, 3, 1, 4)
    vt = v.astype(jnp.float32).transpose(0, 2, 1, 3)
    key_pos = jnp.arange(s)

    def one_block(args):
        q_blk, blk = args
        scores = jnp.einsum('bhcqd,bhckd->bhcqk', q_blk, kt)
        q_pos = blk * Q_BLOCK + jnp.arange(Q_BLOCK)
        causal = key_pos[None, :] <= q_pos[:, None]
        scores = jnp.where(causal, scores, -jnp.inf)
        probs = jax.nn.softmax(scores, axis=-1)
        attn = probs[:, :, 0] - lam * probs[:, :, 1]
        return jnp.einsum('bhqk,bhkd->bhqd', attn, vt)

    out = lax.map(one_block, (qb, jnp.arange(nblk)))
    return out.transpose(1, 0, 3, 2, 4).reshape(b, s, DIFF_HEADS, DIFF_V_DIM)


def s5_ssm(u, lam_re, lam_im, log_dt, b_re, b_im, c_re, c_im, d_skip):
    uf = u.astype(jnp.float32)
    lam = lax.complex(lam_re.astype(jnp.float32), lam_im.astype(jnp.float32))
    dt = jnp.exp(log_dt.astype(jnp.float32))[:, None]
    lam_bar = jnp.exp(lam * dt)
    b_mat = lax.complex(b_re.astype(jnp.float32), b_im.astype(jnp.float32))
    b_bar = ((lam_bar - 1.0) / lam)[..., None] * b_mat
    bu = jnp.einsum('bsgh,gph->bsgp', uf.astype(jnp.complex64), b_bar)
    a = jnp.broadcast_to(lam_bar, bu.shape)

    def combine(left, right):
        a_l, b_l = left
        a_r, b_r = right
        return a_r * a_l, a_r * b_l + b_r

    _, states = lax.associative_scan(combine, (a, bu), axis=1)
    c_mat = lax.complex(c_re.astype(jnp.float32), c_im.astype(jnp.float32))
    y = jnp.einsum('bsgp,ghp->bsgh', states, c_mat).real + d_skip.astype(jnp.float32) * uf
    return y


def setup_inputs(seed: int = 0) -> dict:
    key = jax.random.key(seed)
    ks = iter(jax.random.split(key, 40))
    L = DEPTH

    def nrm(shape, scale):
        return jax.random.normal(next(ks), shape, jnp.float32) * scale

    def gain(shape):
        return 1.0 + nrm(shape, 0.02)

    x = nrm((BATCH, SEQ, D_MODEL), 1.0)
    mem = nrm((BATCH, MEM_LEN, D_MODEL), 1.0)
    g_attn_norm = gain((L, D_MODEL))
    w_in = nrm((L, D_MODEL, IN_COLS), D_MODEL ** -0.5)
    g_q = gain((L, DIFF_QK_DIM))
    g_k = gain((L, DIFF_QK_DIM))
    lam_q1 = nrm((L, DIFF_QK_DIM), 0.1)
    lam_k1 = nrm((L, DIFF_QK_DIM), 0.1)
    lam_q2 = nrm((L, DIFF_QK_DIM), 0.1)
    lam_k2 = nrm((L, DIFF_QK_DIM), 0.1)
    g_subln = gain((L, DIFF_V_DIM))
    lam_re = -0.5 + nrm((L, SSM_GROUPS, SSM_STATE), 0.01)
    lam_im = jnp.broadcast_to(math.pi * jnp.arange(SSM_STATE, dtype=jnp.float32),
                              (L, SSM_GROUPS, SSM_STATE)) + nrm((L, SSM_GROUPS, SSM_STATE), 0.01)
    log_dt = jax.random.uniform(next(ks), (L, SSM_GROUPS), jnp.float32,
                                math.log(DT_MIN), math.log(DT_MAX))
    b_re = nrm((L, SSM_GROUPS, SSM_STATE, SSM_GROUP_CH), (2.0 * SSM_GROUP_CH) ** -0.5)
    b_im = nrm((L, SSM_GROUPS, SSM_STATE, SSM_GROUP_CH), (2.0 * SSM_GROUP_CH) ** -0.5)
    c_re = nrm((L, SSM_GROUPS, SSM_GROUP_CH, SSM_STATE), (2.0 * SSM_STATE) ** -0.5)
    c_im = nrm((L, SSM_GROUPS, SSM_GROUP_CH, SSM_STATE), (2.0 * SSM_STATE) ** -0.5)
    d_skip = nrm((L, SSM_GROUPS, SSM_GROUP_CH), 1.0)
    w_glu = nrm((L, SSM_WIDTH, SSM_WIDTH), SSM_WIDTH ** -0.5)
    b_glu = nrm((L, SSM_WIDTH), 0.02)
    g_ssm_out = gain((L, SSM_WIDTH))
    g_mem = gain((L, D_MODEL))
    w_mem_kv = nrm((L, D_MODEL, 2 * XATTN_WIDTH), D_MODEL ** -0.5)
    g_xq = gain((L, XATTN_HEAD_DIM))
    g_xk = gain((L, XATTN_HEAD_DIM))
    g_xattn_out = gain((L, XATTN_WIDTH))
    w_out = nrm((L, MIX_WIDTH, D_MODEL), MIX_WIDTH ** -0.5)
    g_mlp_norm = gain((L, D_MODEL))
    w_mlp_in = nrm((L, D_MODEL, D_FF), D_MODEL ** -0.5)
    w_mlp_out = nrm((L, D_FF, D_MODEL), D_FF ** -0.5)
    return {"x": x, "mem": mem, "g_attn_norm": g_attn_norm, "w_in": w_in,
            "g_q": g_q, "g_k": g_k, "lam_q1": lam_q1, "lam_k1": lam_k1,
            "lam_q2": lam_q2, "lam_k2": lam_k2, "g_subln": g_subln,
            "lam_re": lam_re, "lam_im": lam_im, "log_dt": log_dt,
            "b_re": b_re, "b_im": b_im, "c_re": c_re, "c_im": c_im, "d_skip": d_skip,
            "w_glu": w_glu, "b_glu": b_glu, "g_ssm_out": g_ssm_out,
            "g_mem": g_mem, "w_mem_kv": w_mem_kv, "g_xq": g_xq, "g_xk": g_xk,
            "g_xattn_out": g_xattn_out, "w_out": w_out, "g_mlp_norm": g_mlp_norm,
            "w_mlp_in": w_mlp_in, "w_mlp_out": w_mlp_out}


def reference(x, mem, g_attn_norm, w_in, g_q, g_k, lam_q1, lam_k1, lam_q2, lam_k2,
              g_subln, lam_re, lam_im, log_dt, b_re, b_im, c_re, c_im, d_skip,
              w_glu, b_glu, g_ssm_out, g_mem, w_mem_kv, g_xq, g_xk, g_xattn_out,
              w_out, g_mlp_norm, w_mlp_in, w_mlp_out):
    b, s = x.shape[0], x.shape[1]
    m = mem.shape[1]
    cos, sin = rotary_tables(s, DIFF_QK_DIM)
    h = x
    for layer in range(DEPTH):
        lambda_init = 0.8 - 0.6 * math.exp(-0.3 * layer)

        hn = rms_norm(h, g_attn_norm[layer])
        proj = hn @ w_in[layer]
        o0 = DIFF_QK_WIDTH
        o1 = o0 + DIFF_QK_WIDTH
        o2 = o1 + DIFF_WIDTH
        o3 = o2 + SSM_WIDTH
        q = proj[..., :o0].reshape(b, s, DIFF_HEADS, 2, DIFF_QK_DIM)
        k = proj[..., o0:o1].reshape(b, s, DIFF_HEADS, 2, DIFF_QK_DIM)
        v = proj[..., o1:o2].reshape(b, s, DIFF_HEADS, DIFF_V_DIM)
        u = proj[..., o2:o3].reshape(b, s, SSM_GROUPS, SSM_GROUP_CH)
        xq = proj[..., o3:].reshape(b, s, XATTN_HEADS, XATTN_HEAD_DIM)

        q = apply_rotary(rms_norm(q, g_q[layer]), cos, sin)
        k = apply_rotary(rms_norm(k, g_k[layer]), cos, sin)
        lam = (jnp.exp(jnp.sum(lam_q1[layer].astype(jnp.float32) * lam_k1[layer].astype(jnp.float32)))
               - jnp.exp(jnp.sum(lam_q2[layer].astype(jnp.float32) * lam_k2[layer].astype(jnp.float32)))
               + lambda_init)
        a_out = diff_attention(q, k, v, lam)
        a_out = rms_norm(a_out, g_subln[layer]) * (1.0 - lambda_init)
        a_out = a_out.reshape(b, s, DIFF_WIDTH)

        y = s5_ssm(u, lam_re[layer], lam_im[layer], log_dt[layer], b_re[layer], b_im[layer],
                   c_re[layer], c_im[layer], d_skip[layer]).reshape(b, s, SSM_WIDTH)
        z = jax.nn.gelu(y)
        z = z * jax.nn.sigmoid(z @ w_glu[layer].astype(jnp.float32) + b_glu[layer].astype(jnp.float32))
        s_out = rms_norm(z, g_ssm_out[layer])

        mem_n = rms_norm(mem, g_mem[layer])
        kv = (mem_n @ w_mem_kv[layer]).reshape(b, m, 2, XATTN_HEADS, XATTN_HEAD_DIM)
        xk = rms_norm(kv[:, :, 0], g_xk[layer]).astype(jnp.float32)
        xv = kv[:, :, 1].astype(jnp.float32)
        xqn = rms_norm(xq, g_xq[layer]).astype(jnp.float32) * (XATTN_HEAD_DIM ** -0.5)
        x_scores = jnp.einsum('bshd,bmhd->bhsm', xqn, xk)
        x_probs = jax.nn.softmax(x_scores, axis=-1)
        x_out = jnp.einsum('bhsm,bmhd->bshd', x_probs, xv).reshape(b, s, XATTN_WIDTH)
        x_out = rms_norm(x_out, g_xattn_out[layer])

        mixed = jnp.concatenate([a_out, s_out, x_out], axis=-1).astype(h.dtype)
        h = h + mixed @ w_out[layer]

        hm = rms_norm(h, g_mlp_norm[layer])
        ff = jnp.square(jax.nn.relu(hm @ w_mlp_in[layer]))
        h = h + ff @ w_mlp_out[layer]
    return h.astype(x.dtype)
```

```python
import functools
import math

import jax
import jax.numpy as jnp
from jax import lax
from jax.experimental import pallas as pl
from jax.experimental.pallas import tpu as pltpu

F32 = jnp.float32
BF16 = jnp.bfloat16

LANES = 128
VMEM_CAP_BYTES = 60000 * 1024

DIFF_HEADS = 8
DIFF_QK_DIM = 64
DIFF_V_DIM = 2 * DIFF_QK_DIM
DIFF_WIDTH = DIFF_HEADS * DIFF_V_DIM
SSM_GROUPS = 32
SSM_GROUP_CH = 16
SSM_STATE = 64
SSM_WIDTH = SSM_GROUPS * SSM_GROUP_CH
XATTN_HEADS = 4
XATTN_HEAD_DIM = 128
XATTN_WIDTH = XATTN_HEADS * XATTN_HEAD_DIM
ROPE_THETA = 10000.0
EPS = 1e-6

SSM_CHUNK = 16
SSM_LANE_GROUPS = LANES // SSM_GROUP_CH
SSM_TILES = SSM_WIDTH // LANES
SSM_TILE_STATE = SSM_LANE_GROUPS * SSM_STATE

ROW_TILE = 512
ATTN_TILE = 512
MLP_FF_TILE = 1024

_NT = (((1,), (1,)), ((), ()))


def _vmem_limit(block_bytes, scratch_bytes=0):
    return int(min(2 * block_bytes + scratch_bytes + (4 << 20), VMEM_CAP_BYTES))


def _rms(t, g):
    ms = jnp.mean(t * t, axis=-1, keepdims=True)
    return t * lax.rsqrt(ms + EPS) * g


def _memkv_kernel(mem_ref, g_ref, w_ref, gxk_ref, k_out, v_out):
    hn = _rms(mem_ref[...], g_ref[...]).astype(BF16)
    kv = jnp.dot(hn, w_ref[...], preferred_element_type=F32)
    gxk = gxk_ref[...]
    for hh in range(XATTN_HEADS):
        sl = slice(hh * XATTN_HEAD_DIM, (hh + 1) * XATTN_HEAD_DIM)
        k_out[:, sl] = _rms(kv[:, sl], gxk).astype(BF16)
    v_out[...] = kv[:, XATTN_WIDTH:].astype(BF16)


def _memkv(mem2d, g_mem, w_kv, g_xk):
    rows, d = mem2d.shape
    blk = rows * d * 4 + d * 2 * XATTN_WIDTH * 2 + 2 * rows * XATTN_WIDTH * 2
    return pl.pallas_call(
        _memkv_kernel,
        grid=(1,),
        in_specs=[
            pl.BlockSpec((rows, d), lambda i: (0, 0)),
            pl.BlockSpec((1, d), lambda i: (0, 0)),
            pl.BlockSpec((d, 2 * XATTN_WIDTH), lambda i: (0, 0)),
            pl.BlockSpec((1, XATTN_HEAD_DIM), lambda i: (0, 0)),
        ],
        out_specs=[
            pl.BlockSpec((rows, XATTN_WIDTH), lambda i: (0, 0)),
            pl.BlockSpec((rows, XATTN_WIDTH), lambda i: (0, 0)),
        ],
        out_shape=[jax.ShapeDtypeStruct((rows, XATTN_WIDTH), BF16)] * 2,
        compiler_params=pltpu.CompilerParams(
            vmem_limit_bytes=_vmem_limit(blk, rows * 2 * XATTN_WIDTH * 8)),
        name="memkv",
    )(mem2d, g_mem, w_kv, g_xk)


def _inproj_kernel(x_ref, gn_ref, w_ref, gq_ref, gk_ref, cos_ref, sin_ref,
                   xk_ref, xv_ref, gxq_ref, gxo_ref,
                   q_out, k_out, v_out, u_out, xo_out, hn_scr):
    j = pl.program_id(1)

    @pl.when(j == 0)
    def _():
        hn_scr[...] = _rms(x_ref[...], gn_ref[...]).astype(BF16)

    acc = jnp.dot(hn_scr[...], w_ref[...], preferred_element_type=F32)
    tm = acc.shape[0]

    def qk_epilogue(g_ref, out_ref, scale):
        lane = lax.broadcasted_iota(jnp.int32, (tm, LANES), 1)
        lo_comp = lane < DIFF_QK_DIM
        first_half = (lane & (DIFF_QK_DIM - 1)) < DIFF_QK_DIM // 2
        cos = cos_ref[...]
        sin = sin_ref[...]
        g = g_ref[...]
        for h in range(DIFF_HEADS):
            sl = slice(h * LANES, (h + 1) * LANES)
            t = acc[:, sl]
            t2 = t * t
            s_lo = jnp.sum(jnp.where(lo_comp, t2, 0.0), axis=-1, keepdims=True)
            s_hi = jnp.sum(jnp.where(lo_comp, 0.0, t2), axis=-1, keepdims=True)
            ms = jnp.where(lo_comp, s_lo, s_hi) * (1.0 / DIFF_QK_DIM)
            tn = t * lax.rsqrt(ms + EPS) * g
            fwd = pltpu.roll(tn, DIFF_QK_DIM // 2, 1)
            bwd = pltpu.roll(tn, LANES - DIFF_QK_DIM // 2, 1)
            rot = jnp.where(first_half, -bwd, fwd)
            out_ref[:, sl] = ((tn * cos + rot * sin) * scale).astype(BF16)

    @pl.when(j == 0)
    def _():
        qk_epilogue(gq_ref, q_out, DIFF_QK_DIM ** -0.5)

    @pl.when(j == 1)
    def _():
        qk_epilogue(gk_ref, k_out, 1.0)

    @pl.when(j == 2)
    def _():
        v_out[...] = acc.astype(BF16)

    @pl.when(j == 3)
    def _():
        for c in range(SSM_TILES):
            u_out[c] = acc[:, c * LANES:(c + 1) * LANES].astype(BF16)
        gxq = gxq_ref[...]
        outs = []
        ssq = jnp.zeros((tm, 1), F32)
        for hh in range(XATTN_HEADS):
            sl = slice(hh * XATTN_HEAD_DIM, (hh + 1) * XATTN_HEAD_DIM)
            t = acc[:, SSM_WIDTH + hh * XATTN_HEAD_DIM:SSM_WIDTH + (hh + 1) * XATTN_HEAD_DIM]
            qn = (_rms(t, gxq) * (XATTN_HEAD_DIM ** -0.5)).astype(BF16)
            s = lax.dot_general(qn, xk_ref[0, :, sl], _NT, preferred_element_type=F32)
            p = jnp.exp(s - jnp.max(s, axis=-1, keepdims=True))
            l = jnp.sum(p, axis=-1, keepdims=True)
            o = jnp.dot(p.astype(BF16), xv_ref[0, :, sl], preferred_element_type=F32) / l
            outs.append(o)
            ssq = ssq + jnp.sum(o * o, axis=-1, keepdims=True)
        inv = lax.rsqrt(ssq * (1.0 / XATTN_WIDTH) + EPS)
        for hh in range(XATTN_HEADS):
            sl = slice(hh * XATTN_HEAD_DIM, (hh + 1) * XATTN_HEAD_DIM)
            xo_out[:, sl] = (outs[hh] * inv * gxo_ref[:, sl]).astype(BF16)


def _inproj(x2d, seq, g_norm, w_in, g_q2, g_k2, cos2, sin2, xk, xv, g_xq, g_xo):
    rows, d = x2d.shape
    tm = ROW_TILE
    tn = DIFF_WIDTH
    n_col = w_in.shape[1] // tn
    tiles_per_seq = seq // tm
    mem_len = xk.shape[1]
    blk = (tm * d * 4 + d * tn * 2 + 2 * tm * LANES * 4 + 2 * mem_len * XATTN_WIDTH * 2
           + 3 * tm * tn * 2 + tm * SSM_WIDTH * 2 + tm * XATTN_WIDTH * 2)
    row = lambda i, j: (i, 0)
    const = lambda i, j: (0, 0)
    return pl.pallas_call(
        _inproj_kernel,
        grid=(rows // tm, n_col),
        in_specs=[
            pl.BlockSpec((tm, d), row),
            pl.BlockSpec((1, d), const),
            pl.BlockSpec((d, tn), lambda i, j: (0, j)),
            pl.BlockSpec((1, LANES), const),
            pl.BlockSpec((1, LANES), const),
            pl.BlockSpec((tm, LANES), lambda i, j: (i % tiles_per_seq, 0)),
            pl.BlockSpec((tm, LANES), lambda i, j: (i % tiles_per_seq, 0)),
            pl.BlockSpec((1, mem_len, XATTN_WIDTH), lambda i, j: (i // tiles_per_seq, 0, 0)),
            pl.BlockSpec((1, mem_len, XATTN_WIDTH), lambda i, j: (i // tiles_per_seq, 0, 0)),
            pl.BlockSpec((1, XATTN_HEAD_DIM), const),
            pl.BlockSpec((1, XATTN_WIDTH), const),
        ],
        out_specs=[
            pl.BlockSpec((tm, tn), row),
            pl.BlockSpec((tm, tn), row),
            pl.BlockSpec((tm, tn), row),
            pl.BlockSpec((SSM_TILES, tm, LANES), lambda i, j: (0, i, 0)),
            pl.BlockSpec((tm, XATTN_WIDTH), row),
        ],
        out_shape=[
            jax.ShapeDtypeStruct((rows, tn), BF16),
            jax.ShapeDtypeStruct((rows, tn), BF16),
            jax.ShapeDtypeStruct((rows, tn), BF16),
            jax.ShapeDtypeStruct((SSM_TILES, rows, LANES), BF16),
            jax.ShapeDtypeStruct((rows, XATTN_WIDTH), BF16),
        ],
        scratch_shapes=[pltpu.VMEM((tm, d), BF16)],
        compiler_params=pltpu.CompilerParams(
            dimension_semantics=("parallel", "arbitrary"),
            vmem_limit_bytes=_vmem_limit(blk, tm * d * 2 + 4 * tm * tn * 4)),
        name="inproj",
    )(x2d, g_norm, w_in, g_q2, g_k2, cos2, sin2, xk, xv, g_xq, g_xo)


def _attn_kernel(lq1_ref, lk1_ref, lq2_ref, lk2_ref, gsub_ref, q_ref, k_ref, v_ref, o_ref,
                 m_scr, l_scr, acc_scr, *, lambda_init):
    qi = pl.program_id(2)
    tq = q_ref.shape[1]
    tk = tq
    q = q_ref[0]
    lane = lax.broadcasted_iota(jnp.int32, (tq, LANES), 1)
    zero = jnp.zeros_like(q)
    qs = (jnp.where(lane < DIFF_QK_DIM, q, zero), jnp.where(lane >= DIFF_QK_DIM, q, zero))

    m_scr[...] = jnp.full(m_scr.shape, -jnp.inf, F32)
    l_scr[...] = jnp.zeros(l_scr.shape, F32)
    acc_scr[...] = jnp.zeros(acc_scr.shape, F32)

    def block(kb, masked):
        off = pl.multiple_of(kb * tk, tk)
        k = k_ref[0, pl.ds(off, tk), :]
        v = v_ref[0, pl.ds(off, tk), :]
        for c in range(2):
            s = lax.dot_general(qs[c], k, _NT, preferred_element_type=F32)
            if masked:
                row = lax.broadcasted_iota(jnp.int32, (tq, tk), 0)
                col = lax.broadcasted_iota(jnp.int32, (tq, tk), 1)
                s = jnp.where(col <= row, s, -jnp.inf)
            m_old = m_scr[c]
            m_new = jnp.maximum(m_old, jnp.max(s, axis=-1, keepdims=True))
            alpha = jnp.exp(m_old - m_new)
            p = jnp.exp(s - m_new)
            l_scr[c] = alpha * l_scr[c] + jnp.sum(p, axis=-1, keepdims=True)
            acc_scr[c] = alpha * acc_scr[c] + jnp.dot(p.astype(BF16), v,
                                                      preferred_element_type=F32)
            m_scr[c] = m_new

    def full_block(kb, carry):
        block(kb, False)
        return carry

    lax.fori_loop(0, qi, full_block, 0)
    block(qi, True)

    lam = (jnp.exp(jnp.sum(lq1_ref[...] * lk1_ref[...], axis=-1, keepdims=True))
           - jnp.exp(jnp.sum(lq2_ref[...] * lk2_ref[...], axis=-1, keepdims=True))
           + lambda_init)
    o = acc_scr[0] / l_scr[0] - lam * (acc_scr[1] / l_scr[1])
    o_ref[0] = (_rms(o, gsub_ref[...]) * (1.0 - lambda_init)).astype(BF16)


def _attention(q, k, v, lq1, lk1, lq2, lk2, g_subln, lambda_init):
    b, s, _ = q.shape
    tq = ATTN_TILE
    vec = pl.BlockSpec((1, DIFF_QK_DIM), lambda bi, h, qi: (0, 0))
    blk = 2 * tq * LANES * 2 + 2 * s * LANES * 2
    return pl.pallas_call(
        functools.partial(_attn_kernel, lambda_init=lambda_init),
        grid=(b, DIFF_HEADS, s // tq),
        in_specs=[
            vec, vec, vec, vec,
            pl.BlockSpec((1, DIFF_V_DIM), lambda bi, h, qi: (0, 0)),
            pl.BlockSpec((1, tq, LANES), lambda bi, h, qi: (bi, qi, h)),
            pl.BlockSpec((1, s, LANES), lambda bi, h, qi: (bi, 0, h)),
            pl.BlockSpec((1, s, LANES), lambda bi, h, qi: (bi, 0, h)),
        ],
        out_specs=pl.BlockSpec((1, tq, LANES), lambda bi, h, qi: (bi, qi, h)),
        out_shape=jax.ShapeDtypeStruct((b, s, DIFF_WIDTH), BF16),
        scratch_shapes=[
            pltpu.VMEM((2, tq, 1), F32),
            pltpu.VMEM((2, tq, 1), F32),
            pltpu.VMEM((2, tq, LANES), F32),
        ],
        compiler_params=pltpu.CompilerParams(
            dimension_semantics=("parallel", "parallel", "arbitrary"),
            vmem_limit_bytes=_vmem_limit(blk, 6 * tq * LANES * 4 + 8 * tq * tq * 4)),
        name="diffattn",
    )(lq1, lk1, lq2, lk2, g_subln, q, k, v)


def _ssm_prep_kernel(lr_row, li_row, ld_row, lr_col, li_col, ld_col,
                     bre_ref, bim_ref, cre_ref, cim_ref, d_ref,
                     m_out, w_out, n_out, a_out):
    t_len = SSM_CHUNK
    ns = SSM_TILE_STATE

    def lam_bar(lr, li, ld):
        dt = jnp.exp(ld)
        e = jnp.exp(lr * dt)
        return e * jnp.cos(li * dt), e * jnp.sin(li * dt)

    lam_r = lr_row[0]
    lam_i = li_row[0]
    br, bi = lam_bar(lam_r, lam_i, ld_row[0])
    den = lam_r * lam_r + lam_i * lam_i
    nr = br - 1.0
    coef_r = (nr * lam_r + bi * lam_i) / den
    coef_i = (bi * lam_r - nr * lam_i) / den
    b_r = bre_ref[0]
    b_i = bim_ref[0]
    bb_r = coef_r * b_r - coef_i * b_i
    bb_i = coef_r * b_i + coef_i * b_r
    pr = jnp.ones_like(br)
    pi = jnp.zeros_like(br)
    for tau in range(t_len):
        rows = slice((t_len - 1 - tau) * LANES, (t_len - tau) * LANES)
        w_out[0, rows, 0:ns] = (bb_r * pr - bb_i * pi).astype(BF16)
        w_out[0, rows, ns:2 * ns] = (bb_r * pi + bb_i * pr).astype(BF16)
        pr, pi = pr * br - pi * bi, pr * bi + pi * br
    a_out[0, :, 0:ns] = pr
    a_out[0, :, ns:2 * ns] = pi

    cr, ci = lam_bar(lr_col[0], li_col[0], ld_col[0])
    c_r = cre_ref[0]
    c_i = cim_ref[0]
    qr = jnp.ones_like(cr)
    qi = jnp.zeros_like(cr)
    eye = (lax.broadcasted_iota(jnp.int32, (LANES, LANES), 0)
           == lax.broadcasted_iota(jnp.int32, (LANES, LANES), 1))
    kblk = []
    for tau in range(t_len + 1):
        cl_r = c_r * qr - c_i * qi
        cl_i = c_r * qi + c_i * qr
        if tau < t_len:
            kb = (jnp.dot(bb_r, cl_r, precision=lax.Precision.HIGHEST, preferred_element_type=F32)
                  - jnp.dot(bb_i, cl_i, precision=lax.Precision.HIGHEST, preferred_element_type=F32))
            if tau == 0:
                kb = kb + jnp.where(eye, d_ref[0], 0.0)
            kblk.append(kb.astype(BF16))
        if tau >= 1:
            cols = slice((tau - 1) * LANES, tau * LANES)
            n_out[0, 0:ns, cols] = cl_r.astype(BF16)
            n_out[0, ns:2 * ns, cols] = (-cl_i).astype(BF16)
        qr, qi = qr * cr - qi * ci, qr * ci + qi * cr
    zeros = jnp.zeros((LANES, LANES), BF16)
    for s in range(t_len):
        for t in range(t_len):
            m_out[0, s * LANES:(s + 1) * LANES, t * LANES:(t + 1) * LANES] = (
                kblk[t - s] if t >= s else zeros)


def _ssm_prep(lam_re, lam_im, log_dt, b_re, b_im, c_re, c_im, d_skip):
    nj, gj, p, h = SSM_TILES, SSM_LANE_GROUPS, SSM_STATE, SSM_GROUP_CH
    ns, tl = SSM_TILE_STATE, SSM_CHUNK * LANES
    eye = jnp.eye(gj, dtype=F32)
    dt_full = jnp.broadcast_to(log_dt[:, None], (SSM_GROUPS, p))

    def expand_b(b):
        b4 = b.reshape(nj, gj, p, h).transpose(0, 1, 3, 2)
        return (b4[:, :, :, None, :] * eye[None, :, None, :, None]).reshape(nj, LANES, ns)

    def expand_c(c):
        c4 = c.reshape(nj, gj, h, p).transpose(0, 1, 3, 2)
        return (c4[:, :, :, None, :] * eye[None, :, None, :, None]).reshape(nj, ns, LANES)

    args = (lam_re.reshape(nj, 1, ns), lam_im.reshape(nj, 1, ns), dt_full.reshape(nj, 1, ns),
            lam_re.reshape(nj, ns, 1), lam_im.reshape(nj, ns, 1), dt_full.reshape(nj, ns, 1),
            expand_b(b_re), expand_b(b_im), expand_c(c_re), expand_c(c_im),
            d_skip.reshape(nj, 1, LANES))
    spec = lambda shp: pl.BlockSpec((1,) + shp, lambda j: (j, 0, 0))
    out_bytes = tl * tl * 2 + 2 * tl * 2 * ns * 2 + 2 * ns * 4
    return pl.pallas_call(
        _ssm_prep_kernel,
        grid=(nj,),
        in_specs=[spec((1, ns))] * 3 + [spec((ns, 1))] * 3 + [spec((LANES, ns))] * 2
                 + [spec((ns, LANES))] * 2 + [spec((1, LANES))],
        out_specs=[spec((tl, tl)), spec((tl, 2 * ns)), spec((2 * ns, tl)), spec((1, 2 * ns))],
        out_shape=[
            jax.ShapeDtypeStruct((nj, tl, tl), BF16),
            jax.ShapeDtypeStruct((nj, tl, 2 * ns), BF16),
            jax.ShapeDtypeStruct((nj, 2 * ns, tl), BF16),
            jax.ShapeDtypeStruct((nj, 1, 2 * ns), F32),
        ],
        compiler_params=pltpu.CompilerParams(
            dimension_semantics=("parallel",),
            vmem_limit_bytes=_vmem_limit(out_bytes + 7 * LANES * ns * 4, 8 << 20)),
        name="ssm_prep",
    )(*args)


def _ssm_sum_kernel(u_ref, w_ref, s_out):
    s_out[0] = jnp.dot(u_ref[0], w_ref[0], preferred_element_type=F32)


def _ssm_sum(u2, w_op):
    nj, rows, tl = u2.shape
    ns2 = w_op.shape[2]
    spec = lambda shp: pl.BlockSpec((1,) + shp, lambda j: (j, 0, 0))
    blk = rows * tl * 2 + tl * ns2 * 2 + rows * ns2 * 4
    return pl.pallas_call(
        _ssm_sum_kernel,
        grid=(nj,),
        in_specs=[spec((rows, tl)), spec((tl, ns2))],
        out_specs=spec((rows, ns2)),
        out_shape=jax.ShapeDtypeStruct((nj, rows, ns2), F32),
        compiler_params=pltpu.CompilerParams(
            dimension_semantics=("parallel",), vmem_limit_bytes=_vmem_limit(blk)),
        name="ssm_sum",
    )(u2, w_op)


def _ssm_scan_kernel(s_ref, a_ref, x_out, *, n_batch, n_chunk):
    ns = SSM_TILE_STATE
    a = a_ref[0]
    ar = a[:, 0:ns]
    ai = a[:, ns:2 * ns]

    def body(c, carry):
        new = []
        for b in range(n_batch):
            xr, xi = carry[2 * b], carry[2 * b + 1]
            r = b * n_chunk + c
            x_out[0, pl.ds(r, 1), 0:ns] = xr
            x_out[0, pl.ds(r, 1), ns:2 * ns] = xi
            s = s_ref[0, pl.ds(r, 1), :]
            new.append(ar * xr - ai * xi + s[:, 0:ns])
            new.append(ar * xi + ai * xr + s[:, ns:2 * ns])
        return tuple(new)

    init = tuple(jnp.zeros((1, ns), F32) for _ in range(2 * n_batch))
    lax.fori_loop(0, n_chunk, body, init)


def _ssm_scan(s_sum, a_pow, n_batch):
    nj, rows, ns2 = s_sum.shape
    spec = lambda shp: pl.BlockSpec((1,) + shp, lambda j: (j, 0, 0))
    return pl.pallas_call(
        functools.partial(_ssm_scan_kernel, n_batch=n_batch, n_chunk=rows // n_batch),
        grid=(nj,),
        in_specs=[spec((rows, ns2)), spec((1, ns2))],
        out_specs=spec((rows, ns2)),
        out_shape=jax.ShapeDtypeStruct((nj, rows, ns2), F32),
        compiler_params=pltpu.CompilerParams(
            dimension_semantics=("parallel",),
            vmem_limit_bytes=_vmem_limit(2 * rows * ns2 * 4)),
        name="ssm_scan",
    )(s_sum, a_pow)


def _gelu_tanh(x):
    return x * (0.5 * (1.0 + jnp.tanh(math.sqrt(2.0 / math.pi) * (x + 0.044715 * (x * x * x)))))


def _ssm_out_kernel(u_ref, m_ref, x_ref, n_ref, z_out):
    y = jnp.dot(u_ref[0], m_ref[0], preferred_element_type=F32)
    y = y + jnp.dot(x_ref[0].astype(BF16), n_ref[0], preferred_element_type=F32)
    z_out[0] = _gelu_tanh(y).astype(BF16)


def _ssm_out(u2, m_op, x_state, n_op):
    nj, rows, tl = u2.shape
    ns2 = x_state.shape[2]
    n_split = 2
    tr = rows // n_split
    rspec = lambda shp: pl.BlockSpec((1,) + shp, lambda j, r: (j, r, 0))
    cspec = lambda shp: pl.BlockSpec((1,) + shp, lambda j, r: (j, 0, 0))
    blk = tr * tl * 2 + tl * tl * 2 + tr * ns2 * 4 + ns2 * tl * 2 + tr * tl * 2
    return pl.pallas_call(
        _ssm_out_kernel,
        grid=(nj, n_split),
        in_specs=[rspec((tr, tl)), cspec((tl, tl)), rspec((tr, ns2)), cspec((ns2, tl))],
        out_specs=rspec((tr, tl)),
        out_shape=jax.ShapeDtypeStruct((nj, rows, tl), BF16),
        compiler_params=pltpu.CompilerParams(
            dimension_semantics=("parallel", "arbitrary"),
            vmem_limit_bytes=_vmem_limit(blk, 3 * tr * tl * 4)),
        name="ssm_out",
    )(u2, m_op, x_state, n_op)


def _outproj_kernel(a_ref, z_ref, xo_ref, x_ref, wglu_ref, bglu_ref, gs_ref, wo_ref, gm_ref,
                    h_out, hm_out):
    z = jnp.concatenate([z_ref[c] for c in range(SSM_TILES)], axis=-1)
    gate_in = jnp.dot(z, wglu_ref[...], preferred_element_type=F32) + bglu_ref[...]
    zf = z.astype(F32)
    zg = zf * (1.0 / (1.0 + jnp.exp(-gate_in)))
    s_out = _rms(zg, gs_ref[...]).astype(BF16)
    acc = jnp.dot(a_ref[...], wo_ref[0:DIFF_WIDTH, :], preferred_element_type=F32)
    acc = acc + jnp.dot(s_out, wo_ref[DIFF_WIDTH:DIFF_WIDTH + SSM_WIDTH, :],
                        preferred_element_type=F32)
    acc = acc + jnp.dot(xo_ref[...], wo_ref[DIFF_WIDTH + SSM_WIDTH:, :],
                        preferred_element_type=F32)
    h = x_ref[...] + acc
    h_out[...] = h
    hm_out[...] = _rms(h, gm_ref[...]).astype(BF16)


def _outproj(a_out, z, xo, x2d, w_glu, b_glu, g_ssm, w_out, g_mlp):
    rows, d = x2d.shape
    tm = ROW_TILE
    mix = w_out.shape[0]
    row = lambda i: (i, 0)
    const = lambda i: (0, 0)
    blk = (tm * DIFF_WIDTH * 2 + tm * SSM_WIDTH * 2 + tm * XATTN_WIDTH * 2 + tm * d * 4
           + SSM_WIDTH * SSM_WIDTH * 2 + mix * d * 2 + tm * d * 4 + tm * d * 2)
    return pl.pallas_call(
        _outproj_kernel,
        grid=(rows // tm,),
        in_specs=[
            pl.BlockSpec((tm, DIFF_WIDTH), row),
            pl.BlockSpec((SSM_TILES, tm, LANES), lambda i: (0, i, 0)),
            pl.BlockSpec((tm, XATTN_WIDTH), row),
            pl.BlockSpec((tm, d), row),
            pl.BlockSpec((SSM_WIDTH, SSM_WIDTH), const),
            pl.BlockSpec((1, SSM_WIDTH), const),
            pl.BlockSpec((1, SSM_WIDTH), const),
            pl.BlockSpec((mix, d), const),
            pl.BlockSpec((1, d), const),
        ],
        out_specs=[pl.BlockSpec((tm, d), row), pl.BlockSpec((tm, d), row)],
        out_shape=[jax.ShapeDtypeStruct((rows, d), F32), jax.ShapeDtypeStruct((rows, d), BF16)],
        compiler_params=pltpu.CompilerParams(
            dimension_semantics=("parallel",),
            vmem_limit_bytes=_vmem_limit(blk, 3 * tm * d * 4)),
        name="outproj",
    )(a_out, z, xo, x2d, w_glu, b_glu, g_ssm, w_out, g_mlp)


def _mlp_kernel(hm_ref, w1_ref, w2_ref, h_ref, o_ref):
    f = pl.program_id(1)

    @pl.when(f == 0)
    def _():
        o_ref[...] = h_ref[...]

    ff = jnp.dot(hm_ref[...], w1_ref[...], preferred_element_type=F32)
    ff = jnp.square(jnp.maximum(ff, 0.0)).astype(BF16)
    o_ref[...] += jnp.dot(ff, w2_ref[...], preferred_element_type=F32)


def _mlp(hm, h, w1, w2):
    rows, d = h.shape
    d_ff = w1.shape[1]
    tm, tf = ROW_TILE, MLP_FF_TILE
    blk = tm * d * 2 + d * tf * 2 + tf * d * 2 + tm * d * 4 + tm * d * 4
    return pl.pallas_call(
        _mlp_kernel,
        grid=(rows // tm, d_ff // tf),
        in_specs=[
            pl.BlockSpec((tm, d), lambda i, f: (i, 0)),
            pl.BlockSpec((d, tf), lambda i, f: (0, f)),
            pl.BlockSpec((tf, d), lambda i, f: (f, 0)),
            pl.BlockSpec((tm, d), lambda i, f: (i, 0)),
        ],
        out_specs=pl.BlockSpec((tm, d), lambda i, f: (i, 0)),
        out_shape=jax.ShapeDtypeStruct((rows, d), F32),
        compiler_params=pltpu.CompilerParams(
            dimension_semantics=("parallel", "arbitrary"),
            vmem_limit_bytes=_vmem_limit(blk, tm * tf * 6 + tm * d * 4)),
        name="mlp",
    )(hm, w1, w2, h)


def _rotary_tables(seq_len):
    dim = DIFF_QK_DIM
    inv_freq = 1.0 / (ROPE_THETA ** (jnp.arange(0, dim, 2, dtype=F32) / dim))
    freqs = jnp.arange(seq_len, dtype=F32)[:, None] * inv_freq[None, :]
    emb = jnp.concatenate([freqs, freqs, freqs, freqs], axis=-1)
    return jnp.cos(emb), jnp.sin(emb)


def kernel(x, mem, g_attn_norm, w_in, g_q, g_k, lam_q1, lam_k1, lam_q2, lam_k2, g_subln, lam_re, lam_im, log_dt, b_re, b_im, c_re, c_im, d_skip, w_glu, b_glu, g_ssm_out, g_mem, w_mem_kv, g_xq, g_xk, g_xattn_out, w_out, g_mlp_norm, w_mlp_in, w_mlp_out):
    b, s, d = x.shape
    m = mem.shape[1]
    depth = w_in.shape[0]
    rows = b * s
    cos2, sin2 = _rotary_tables(s)
    h = x.reshape(rows, d)
    mem2d = mem.reshape(b * m, d)
    row2 = lambda v: v.reshape(1, -1)
    for layer in range(depth):
        lambda_init = 0.8 - 0.6 * math.exp(-0.3 * layer)

        xk, xv = _memkv(mem2d, row2(g_mem[layer]), w_mem_kv[layer].astype(BF16),
                        row2(g_xk[layer]))
        q, k, v, u, x_out = _inproj(
            h, s, row2(g_attn_norm[layer]), w_in[layer].astype(BF16),
            row2(jnp.tile(g_q[layer], 2)), row2(jnp.tile(g_k[layer], 2)), cos2, sin2,
            xk.reshape(b, m, XATTN_WIDTH), xv.reshape(b, m, XATTN_WIDTH),
            row2(g_xq[layer]), row2(g_xattn_out[layer]))

        a_out = _attention(q.reshape(b, s, DIFF_WIDTH), k.reshape(b, s, DIFF_WIDTH),
                           v.reshape(b, s, DIFF_WIDTH), row2(lam_q1[layer]), row2(lam_k1[layer]),
                           row2(lam_q2[layer]), row2(lam_k2[layer]), row2(g_subln[layer]),
                           lambda_init)

        m_op, w_op, n_op, a_pow = _ssm_prep(lam_re[layer], lam_im[layer], log_dt[layer],
                                            b_re[layer], b_im[layer], c_re[layer], c_im[layer],
                                            d_skip[layer])
        u2 = u.reshape(SSM_TILES, rows // SSM_CHUNK, SSM_CHUNK * LANES)
        x_state = _ssm_scan(_ssm_sum(u2, w_op), a_pow, b)
        z = _ssm_out(u2, m_op, x_state, n_op).reshape(SSM_TILES, rows, LANES)

        h, hm = _outproj(a_out.reshape(rows, DIFF_WIDTH), z, x_out, h,
                         w_glu[layer].astype(BF16), row2(b_glu[layer]), row2(g_ssm_out[layer]),
                         w_out[layer].astype(BF16), row2(g_mlp_norm[layer]))
        h = _mlp(hm, h, w_mlp_in[layer].astype(BF16), w_mlp_out[layer].astype(BF16))
    return h.reshape(b, s, d)
```

```python
import functools
import math

import jax
import jax.numpy as jnp
from jax import lax
from jax.experimental import pallas as pl
from jax.experimental.pallas import tpu as pltpu

F32 = jnp.float32
BF16 = jnp.bfloat16

LANES = 128
VMEM_CAP_BYTES = 60000 * 1024

DIFF_HEADS = 8
DIFF_QK_DIM = 64
DIFF_V_DIM = 2 * DIFF_QK_DIM
DIFF_WIDTH = DIFF_HEADS * DIFF_V_DIM
SSM_GROUPS = 32
SSM_GROUP_CH = 16
SSM_STATE = 64
SSM_WIDTH = SSM_GROUPS * SSM_GROUP_CH
XATTN_HEADS = 4
XATTN_HEAD_DIM = 128
XATTN_WIDTH = XATTN_HEADS * XATTN_HEAD_DIM
ROPE_THETA = 10000.0
EPS = 1e-6
LOG2_E = math.log2(math.e)

SSM_CHUNK = 16
SSM_LANE_GROUPS = LANES // SSM_GROUP_CH
SSM_TILES = SSM_WIDTH // LANES
SSM_TILE_STATE = SSM_LANE_GROUPS * SSM_STATE

ROW_TILE = 512
ATTN_TILE = 512
ATTN_KEY_TILE = 256
MLP_FF_TILE = 1024

_NT = (((1,), (1,)), ((), ()))


def _vmem_limit(block_bytes, scratch_bytes=0):
    return int(min(2 * block_bytes + scratch_bytes + (4 << 20), VMEM_CAP_BYTES))


def _rms(t, g):
    ms = jnp.mean(t * t, axis=-1, keepdims=True)
    return t * lax.rsqrt(ms + EPS) * g


def _memkv_kernel(mem_ref, g_ref, w_ref, gxk_ref, k_out, v_out):
    hn = _rms(mem_ref[...], g_ref[...]).astype(BF16)
    kv = jnp.dot(hn, w_ref[...], preferred_element_type=F32)
    gxk = gxk_ref[...]
    for hh in range(XATTN_HEADS):
        sl = slice(hh * XATTN_HEAD_DIM, (hh + 1) * XATTN_HEAD_DIM)
        k_out[:, sl] = _rms(kv[:, sl], gxk).astype(BF16)
    v_out[...] = kv[:, XATTN_WIDTH:].astype(BF16)


def _memkv(mem2d, g_mem, w_kv, g_xk):
    rows, d = mem2d.shape
    blk = rows * d * 4 + d * 2 * XATTN_WIDTH * 2 + 2 * rows * XATTN_WIDTH * 2
    return pl.pallas_call(
        _memkv_kernel,
        grid=(1,),
        in_specs=[
            pl.BlockSpec((rows, d), lambda i: (0, 0)),
            pl.BlockSpec((1, d), lambda i: (0, 0)),
            pl.BlockSpec((d, 2 * XATTN_WIDTH), lambda i: (0, 0)),
            pl.BlockSpec((1, XATTN_HEAD_DIM), lambda i: (0, 0)),
        ],
        out_specs=[
            pl.BlockSpec((rows, XATTN_WIDTH), lambda i: (0, 0)),
            pl.BlockSpec((rows, XATTN_WIDTH), lambda i: (0, 0)),
        ],
        out_shape=[jax.ShapeDtypeStruct((rows, XATTN_WIDTH), BF16)] * 2,
        compiler_params=pltpu.CompilerParams(
            vmem_limit_bytes=_vmem_limit(blk, rows * 2 * XATTN_WIDTH * 8)),
        name="memkv",
    )(mem2d, g_mem, w_kv, g_xk)


def _inproj_kernel(x_ref, gn_ref, w_ref, gq_ref, gk_ref, cos_ref, sin_ref,
                   xk_ref, xv_ref, gxq_ref, gxo_ref,
                   q_out, k_out, v_out, u_out, xo_out):
    hn = _rms(x_ref[...], gn_ref[...]).astype(BF16)
    tm = hn.shape[0]
    tn = DIFF_WIDTH

    def project(j):
        return jnp.dot(hn, w_ref[:, j * tn:(j + 1) * tn], preferred_element_type=F32)

    def qk_epilogue(acc, g_ref, out_ref, scale):
        lane = lax.broadcasted_iota(jnp.int32, (tm, LANES), 1)
        lo_comp = lane < DIFF_QK_DIM
        first_half = (lane & (DIFF_QK_DIM - 1)) < DIFF_QK_DIM // 2
        cos = cos_ref[...]
        sin = sin_ref[...]
        g = g_ref[...]
        for h in range(DIFF_HEADS):
            sl = slice(h * LANES, (h + 1) * LANES)
            t = acc[:, sl]
            t2 = t * t
            s_lo = jnp.sum(jnp.where(lo_comp, t2, 0.0), axis=-1, keepdims=True)
            s_hi = jnp.sum(jnp.where(lo_comp, 0.0, t2), axis=-1, keepdims=True)
            ms = jnp.where(lo_comp, s_lo, s_hi) * (1.0 / DIFF_QK_DIM)
            tn_ = t * lax.rsqrt(ms + EPS) * g
            fwd = pltpu.roll(tn_, DIFF_QK_DIM // 2, 1)
            bwd = pltpu.roll(tn_, LANES - DIFF_QK_DIM // 2, 1)
            rot = jnp.where(first_half, -bwd, fwd)
            out_ref[:, sl] = ((tn_ * cos + rot * sin) * scale).astype(BF16)

    qk_epilogue(project(0), gq_ref, q_out, DIFF_QK_DIM ** -0.5 * LOG2_E)
    qk_epilogue(project(1), gk_ref, k_out, 1.0)
    v_out[0] = project(2).T.astype(BF16)

    acc = project(3)
    for c in range(SSM_TILES):
        u_out[c] = acc[:, c * LANES:(c + 1) * LANES].astype(BF16)
    gxq = gxq_ref[...]
    outs = []
    ssq = jnp.zeros((tm, 1), F32)
    for hh in range(XATTN_HEADS):
        sl = slice(hh * XATTN_HEAD_DIM, (hh + 1) * XATTN_HEAD_DIM)
        t = acc[:, SSM_WIDTH + hh * XATTN_HEAD_DIM:SSM_WIDTH + (hh + 1) * XATTN_HEAD_DIM]
        qn = (_rms(t, gxq) * (XATTN_HEAD_DIM ** -0.5)).astype(BF16)
        s = lax.dot_general(qn, xk_ref[0, :, sl], _NT, preferred_element_type=F32)
        p = jnp.exp(s - jnp.max(s, axis=-1, keepdims=True))
        l = jnp.sum(p, axis=-1, keepdims=True)
        o = jnp.dot(p.astype(BF16), xv_ref[0, :, sl], preferred_element_type=F32) / l
        outs.append(o)
        ssq = ssq + jnp.sum(o * o, axis=-1, keepdims=True)
    inv = lax.rsqrt(ssq * (1.0 / XATTN_WIDTH) + EPS)
    for hh in range(XATTN_HEADS):
        sl = slice(hh * XATTN_HEAD_DIM, (hh + 1) * XATTN_HEAD_DIM)
        xo_out[:, sl] = (outs[hh] * inv * gxo_ref[:, sl]).astype(BF16)


def _inproj(x2d, seq, g_norm, w_in, g_q2, g_k2, cos2, sin2, xk, xv, g_xq, g_xo):
    rows, d = x2d.shape
    tm = ROW_TILE
    tn = DIFF_WIDTH
    n_in = w_in.shape[1]
    tiles_per_seq = seq // tm
    mem_len = xk.shape[1]
    blk = (tm * d * 4 + 2 * tm * LANES * 4 + 2 * mem_len * XATTN_WIDTH * 2
           + 3 * tm * tn * 2 + tm * SSM_WIDTH * 2 + tm * XATTN_WIDTH * 2)
    row = lambda i: (i, 0)
    const = lambda i: (0, 0)
    return pl.pallas_call(
        _inproj_kernel,
        grid=(rows // tm,),
        in_specs=[
            pl.BlockSpec((tm, d), row),
            pl.BlockSpec((1, d), const),
            pl.BlockSpec((d, n_in), const, pipeline_mode=pl.Buffered(1)),
            pl.BlockSpec((1, LANES), const),
            pl.BlockSpec((1, LANES), const),
            pl.BlockSpec((tm, LANES), lambda i: (i % tiles_per_seq, 0)),
            pl.BlockSpec((tm, LANES), lambda i: (i % tiles_per_seq, 0)),
            pl.BlockSpec((1, mem_len, XATTN_WIDTH), lambda i: (i // tiles_per_seq, 0, 0)),
            pl.BlockSpec((1, mem_len, XATTN_WIDTH), lambda i: (i // tiles_per_seq, 0, 0)),
            pl.BlockSpec((1, XATTN_HEAD_DIM), const),
            pl.BlockSpec((1, XATTN_WIDTH), const),
        ],
        out_specs=[
            pl.BlockSpec((tm, tn), row),
            pl.BlockSpec((tm, tn), row),
            pl.BlockSpec((1, tn, tm), lambda i: (i // tiles_per_seq, 0, i % tiles_per_seq)),
            pl.BlockSpec((SSM_TILES, tm, LANES), lambda i: (0, i, 0)),
            pl.BlockSpec((tm, XATTN_WIDTH), row),
        ],
        out_shape=[
            jax.ShapeDtypeStruct((rows, tn), BF16),
            jax.ShapeDtypeStruct((rows, tn), BF16),
            jax.ShapeDtypeStruct((rows // seq, tn, seq), BF16),
            jax.ShapeDtypeStruct((SSM_TILES, rows, LANES), BF16),
            jax.ShapeDtypeStruct((rows, XATTN_WIDTH), BF16),
        ],
        compiler_params=pltpu.CompilerParams(
            dimension_semantics=("parallel",),
            vmem_limit_bytes=_vmem_limit(blk, d * n_in * 2 + tm * d * 2 + 6 * tm * tn * 4)),
        name="inproj",
    )(x2d, g_norm, w_in, g_q2, g_k2, cos2, sin2, xk, xv, g_xq, g_xo)


def _attn_kernel(lq1_ref, lk1_ref, lq2_ref, lk2_ref, gsub_ref, q_ref, k_ref, vt_ref, o_ref,
                 s_scr, acc_scr, *, lambda_init):
    qi = pl.program_id(2)
    tq = q_ref.shape[1]
    tk = ATTN_KEY_TILE
    blocks_per_q = tq // tk
    q = q_ref[0]
    lane = lax.broadcasted_iota(jnp.int32, (tq, LANES), 1)
    zero = jnp.zeros_like(q)
    qs = (jnp.where(lane < DIFF_QK_DIM, q, zero), jnp.where(lane >= DIFF_QK_DIM, q, zero))

    acc_scr[...] = jnp.zeros(acc_scr.shape, F32)

    def scores(kb, slot):
        k = k_ref[0, pl.ds(pl.multiple_of(kb * tk, tk), tk), :]
        for c in range(2):
            s_scr[slot, c] = lax.dot_general(k, qs[c], _NT, preferred_element_type=F32)

    def softmax_pv(kb, slot, stats, diag_index=None):
        vt = vt_ref[0, :, pl.ds(pl.multiple_of(kb * tk, tk), tk)]
        new_stats = []
        for c in range(2):
            m_old, l_old = stats[2 * c], stats[2 * c + 1]
            s = s_scr[slot, c]
            if diag_index is not None:
                key = lax.broadcasted_iota(jnp.int32, (tk, tq), 0) + diag_index * tk
                qry = lax.broadcasted_iota(jnp.int32, (tk, tq), 1)
                s = jnp.where(key <= qry, s, -jnp.inf)
            m_new = jnp.maximum(m_old, jnp.max(s, axis=0, keepdims=True))
            alpha = jnp.exp2(m_old - m_new)
            p = jnp.exp2(s - m_new)
            l_new = alpha * l_old + jnp.sum(p, axis=0, keepdims=True)
            acc_scr[c] = alpha * acc_scr[c] + jnp.dot(vt, p.astype(BF16),
                                                      preferred_element_type=F32)
            new_stats += [m_new, l_new]
        return tuple(new_stats)

    def pair(i, stats):
        kb = blocks_per_q * i
        for j in range(blocks_per_q):
            scores(kb + j + 1, (j + 1) % 2)
            stats = softmax_pv(kb + j, j % 2, stats)
        return stats

    init = (jnp.full((1, tq), -jnp.inf, F32), jnp.zeros((1, tq), F32)) * 2
    scores(0, 0)
    stats = lax.fori_loop(0, qi, pair, init)
    kb = blocks_per_q * qi
    for j in range(blocks_per_q):
        if j + 1 < blocks_per_q:
            scores(kb + j + 1, (j + 1) % 2)
        stats = softmax_pv(kb + j, j % 2, stats, diag_index=j)

    lam = (jnp.exp(jnp.sum(lq1_ref[...] * lk1_ref[...], axis=-1, keepdims=True))
           - jnp.exp(jnp.sum(lq2_ref[...] * lk2_ref[...], axis=-1, keepdims=True))
           + lambda_init)
    o_t = acc_scr[0] / stats[1] - lam * (acc_scr[1] / stats[3])
    ms = jnp.mean(o_t * o_t, axis=0, keepdims=True)
    o_t = o_t * lax.rsqrt(ms + EPS) * (gsub_ref[...] * (1.0 - lambda_init))
    o_ref[0] = o_t.T.astype(BF16)


def _attention(q, k, vt, lq1, lk1, lq2, lk2, g_subln_col, lambda_init):
    b, s, _ = q.shape
    tq = ATTN_TILE
    tk = ATTN_KEY_TILE
    assert tq % (2 * tk) == 0
    vec = pl.BlockSpec((1, DIFF_QK_DIM), lambda bi, h, qi: (0, 0))
    blk = 2 * tq * LANES * 2 + 2 * s * LANES * 2
    return pl.pallas_call(
        functools.partial(_attn_kernel, lambda_init=lambda_init),
        grid=(b, DIFF_HEADS, s // tq),
        in_specs=[
            vec, vec, vec, vec,
            pl.BlockSpec((DIFF_V_DIM, 1), lambda bi, h, qi: (0, 0)),
            pl.BlockSpec((1, tq, LANES), lambda bi, h, qi: (bi, qi, h)),
            pl.BlockSpec((1, s, LANES), lambda bi, h, qi: (bi, 0, h)),
            pl.BlockSpec((1, DIFF_V_DIM, s), lambda bi, h, qi: (bi, h, 0)),
        ],
        out_specs=pl.BlockSpec((1, tq, LANES), lambda bi, h, qi: (bi, qi, h)),
        out_shape=jax.ShapeDtypeStruct((b, s, DIFF_WIDTH), BF16),
        scratch_shapes=[
            pltpu.VMEM((2, 2, tk, tq), F32),
            pltpu.VMEM((2, DIFF_V_DIM, tq), F32),
        ],
        compiler_params=pltpu.CompilerParams(
            dimension_semantics=("parallel", "parallel", "arbitrary"),
            vmem_limit_bytes=_vmem_limit(blk, 2 * DIFF_V_DIM * tq * 4 + 12 * tk * tq * 4)),
        name="diffattn",
    )(lq1, lk1, lq2, lk2, g_subln_col, q, k, vt)


def _ssm_prep_kernel(lr_row, li_row, ld_row, lr_col, li_col, ld_col,
                     bre_ref, bim_ref, cre_ref, cim_ref, d_ref,
                     m_out, w_out, n_out, a_out):
    t_len = SSM_CHUNK
    ns = SSM_TILE_STATE

    def lam_bar(lr, li, ld):
        dt = jnp.exp(ld)
        e = jnp.exp(lr * dt)
        return e * jnp.cos(li * dt), e * jnp.sin(li * dt)

    lam_r = lr_row[0]
    lam_i = li_row[0]
    br, bi = lam_bar(lam_r, lam_i, ld_row[0])
    den = lam_r * lam_r + lam_i * lam_i
    nr = br - 1.0
    coef_r = (nr * lam_r + bi * lam_i) / den
    coef_i = (bi * lam_r - nr * lam_i) / den
    b_r = bre_ref[0]
    b_i = bim_ref[0]
    bb_r = coef_r * b_r - coef_i * b_i
    bb_i = coef_r * b_i + coef_i * b_r
    pr = jnp.ones_like(br)
    pi = jnp.zeros_like(br)
    for tau in range(t_len):
        rows = slice((t_len - 1 - tau) * LANES, (t_len - tau) * LANES)
        w_out[0, rows, 0:ns] = (bb_r * pr - bb_i * pi).astype(BF16)
        w_out[0, rows, ns:2 * ns] = (bb_r * pi + bb_i * pr).astype(BF16)
        pr, pi = pr * br - pi * bi, pr * bi + pi * br
    a_out[0, :, 0:ns] = pr
    a_out[0, :, ns:2 * ns] = pi

    cr, ci = lam_bar(lr_col[0], li_col[0], ld_col[0])
    c_r = cre_ref[0]
    c_i = cim_ref[0]
    qr = jnp.ones_like(cr)
    qi = jnp.zeros_like(cr)
    eye = (lax.broadcasted_iota(jnp.int32, (LANES, LANES), 0)
           == lax.broadcasted_iota(jnp.int32, (LANES, LANES), 1))
    kblk = []
    for tau in range(t_len + 1):
        cl_r = c_r * qr - c_i * qi
        cl_i = c_r * qi + c_i * qr
        if tau < t_len:
            kb = (jnp.dot(bb_r, cl_r, precision=lax.Precision.HIGHEST, preferred_element_type=F32)
                  - jnp.dot(bb_i, cl_i, precision=lax.Precision.HIGHEST, preferred_element_type=F32))
            if tau == 0:
                kb = kb + jnp.where(eye, d_ref[0], 0.0)
            kblk.append(kb.astype(BF16))
        if tau >= 1:
            cols = slice((tau - 1) * LANES, tau * LANES)
            n_out[0, 0:ns, cols] = cl_r.astype(BF16)
            n_out[0, ns:2 * ns, cols] = (-cl_i).astype(BF16)
        qr, qi = qr * cr - qi * ci, qr * ci + qi * cr
    zeros = jnp.zeros((LANES, LANES), BF16)
    for s in range(t_len):
        for t in range(t_len):
            m_out[0, s * LANES:(s + 1) * LANES, t * LANES:(t + 1) * LANES] = (
                kblk[t - s] if t >= s else zeros)


def _ssm_prep(lam_re, lam_im, log_dt, b_re, b_im, c_re, c_im, d_skip):
    nj, gj, p, h = SSM_TILES, SSM_LANE_GROUPS, SSM_STATE, SSM_GROUP_CH
    ns, tl = SSM_TILE_STATE, SSM_CHUNK * LANES
    eye = jnp.eye(gj, dtype=F32)
    dt_full = jnp.broadcast_to(log_dt[:, None], (SSM_GROUPS, p))

    def expand_b(b):
        b4 = b.reshape(nj, gj, p, h).transpose(0, 1, 3, 2)
        return (b4[:, :, :, None, :] * eye[None, :, None, :, None]).reshape(nj, LANES, ns)

    def expand_c(c):
        c4 = c.reshape(nj, gj, h, p).transpose(0, 1, 3, 2)
        return (c4[:, :, :, None, :] * eye[None, :, None, :, None]).reshape(nj, ns, LANES)

    args = (lam_re.reshape(nj, 1, ns), lam_im.reshape(nj, 1, ns), dt_full.reshape(nj, 1, ns),
            lam_re.reshape(nj, ns, 1), lam_im.reshape(nj, ns, 1), dt_full.reshape(nj, ns, 1),
            expand_b(b_re), expand_b(b_im), expand_c(c_re), expand_c(c_im),
            d_skip.reshape(nj, 1, LANES))
    spec = lambda shp: pl.BlockSpec((1,) + shp, lambda j: (j, 0, 0))
    out_bytes = tl * tl * 2 + 2 * tl * 2 * ns * 2 + 2 * ns * 4
    return pl.pallas_call(
        _ssm_prep_kernel,
        grid=(nj,),
        in_specs=[spec((1, ns))] * 3 + [spec((ns, 1))] * 3 + [spec((LANES, ns))] * 2
                 + [spec((ns, LANES))] * 2 + [spec((1, LANES))],
        out_specs=[spec((tl, tl)), spec((tl, 2 * ns)), spec((2 * ns, tl)), spec((1, 2 * ns))],
        out_shape=[
            jax.ShapeDtypeStruct((nj, tl, tl), BF16),
            jax.ShapeDtypeStruct((nj, tl, 2 * ns), BF16),
            jax.ShapeDtypeStruct((nj, 2 * ns, tl), BF16),
            jax.ShapeDtypeStruct((nj, 1, 2 * ns), F32),
        ],
        compiler_params=pltpu.CompilerParams(
            dimension_semantics=("parallel",),
            vmem_limit_bytes=_vmem_limit(out_bytes + 7 * LANES * ns * 4, 8 << 20)),
        name="ssm_prep",
    )(*args)


def _ssm_sum_kernel(u_ref, w_ref, s_out):
    s_out[0] = jnp.dot(u_ref[0], w_ref[0], preferred_element_type=F32)


def _ssm_sum(u2, w_op):
    nj, rows, tl = u2.shape
    ns2 = w_op.shape[2]
    spec = lambda shp: pl.BlockSpec((1,) + shp, lambda j: (j, 0, 0))
    blk = rows * tl * 2 + tl * ns2 * 2 + rows * ns2 * 4
    return pl.pallas_call(
        _ssm_sum_kernel,
        grid=(nj,),
        in_specs=[spec((rows, tl)), spec((tl, ns2))],
        out_specs=spec((rows, ns2)),
        out_shape=jax.ShapeDtypeStruct((nj, rows, ns2), F32),
        compiler_params=pltpu.CompilerParams(
            dimension_semantics=("parallel",), vmem_limit_bytes=_vmem_limit(blk)),
        name="ssm_sum",
    )(u2, w_op)


def _ssm_scan_kernel(s_ref, a_ref, x_out, *, n_batch, n_chunk):
    ns = SSM_TILE_STATE
    a = a_ref[0]
    ar = a[:, 0:ns]
    ai = a[:, ns:2 * ns]

    def body(c, carry):
        new = []
        for b in range(n_batch):
            xr, xi = carry[2 * b], carry[2 * b + 1]
            r = b * n_chunk + c
            x_out[0, pl.ds(r, 1), 0:ns] = xr
            x_out[0, pl.ds(r, 1), ns:2 * ns] = xi
            s = s_ref[0, pl.ds(r, 1), :]
            new.append(ar * xr - ai * xi + s[:, 0:ns])
            new.append(ar * xi + ai * xr + s[:, ns:2 * ns])
        return tuple(new)

    init = tuple(jnp.zeros((1, ns), F32) for _ in range(2 * n_batch))
    lax.fori_loop(0, n_chunk, body, init)


def _ssm_scan(s_sum, a_pow, n_batch):
    nj, rows, ns2 = s_sum.shape
    spec = lambda shp: pl.BlockSpec((1,) + shp, lambda j: (j, 0, 0))
    return pl.pallas_call(
        functools.partial(_ssm_scan_kernel, n_batch=n_batch, n_chunk=rows // n_batch),
        grid=(nj,),
        in_specs=[spec((rows, ns2)), spec((1, ns2))],
        out_specs=spec((rows, ns2)),
        out_shape=jax.ShapeDtypeStruct((nj, rows, ns2), F32),
        compiler_params=pltpu.CompilerParams(
            dimension_semantics=("parallel",),
            vmem_limit_bytes=_vmem_limit(2 * rows * ns2 * 4)),
        name="ssm_scan",
    )(s_sum, a_pow)


def _gelu_tanh(x):
    return x * (0.5 * (1.0 + jnp.tanh(math.sqrt(2.0 / math.pi) * (x + 0.044715 * (x * x * x)))))


def _ssm_out_kernel(u_ref, m_ref, x_ref, n_ref, z_out):
    y = jnp.dot(u_ref[0], m_ref[0], preferred_element_type=F32)
    y = y + jnp.dot(x_ref[0].astype(BF16), n_ref[0], preferred_element_type=F32)
    z_out[0] = _gelu_tanh(y).astype(BF16)


def _ssm_out(u2, m_op, x_state, n_op):
    nj, rows, tl = u2.shape
    ns2 = x_state.shape[2]
    n_split = 2
    tr = rows // n_split
    rspec = lambda shp: pl.BlockSpec((1,) + shp, lambda j, r: (j, r, 0))
    cspec = lambda shp: pl.BlockSpec((1,) + shp, lambda j, r: (j, 0, 0))
    blk = tr * tl * 2 + tl * tl * 2 + tr * ns2 * 4 + ns2 * tl * 2 + tr * tl * 2
    return pl.pallas_call(
        _ssm_out_kernel,
        grid=(nj, n_split),
        in_specs=[rspec((tr, tl)), cspec((tl, tl)), rspec((tr, ns2)), cspec((ns2, tl))],
        out_specs=rspec((tr, tl)),
        out_shape=jax.ShapeDtypeStruct((nj, rows, tl), BF16),
        compiler_params=pltpu.CompilerParams(
            dimension_semantics=("parallel", "arbitrary"),
            vmem_limit_bytes=_vmem_limit(blk, 3 * tr * tl * 4)),
        name="ssm_out",
    )(u2, m_op, x_state, n_op)


def _outproj_kernel(a_ref, z_ref, xo_ref, x_ref, wglu_ref, bglu_ref, gs_ref, wo_ref, gm_ref,
                    h_out, hm_out):
    z = jnp.concatenate([z_ref[c] for c in range(SSM_TILES)], axis=-1)
    gate_in = jnp.dot(z, wglu_ref[...], preferred_element_type=F32) + bglu_ref[...]
    zf = z.astype(F32)
    zg = zf * (1.0 / (1.0 + jnp.exp(-gate_in)))
    s_out = _rms(zg, gs_ref[...]).astype(BF16)
    acc = jnp.dot(a_ref[...], wo_ref[0:DIFF_WIDTH, :], preferred_element_type=F32)
    acc = acc + jnp.dot(s_out, wo_ref[DIFF_WIDTH:DIFF_WIDTH + SSM_WIDTH, :],
                        preferred_element_type=F32)
    acc = acc + jnp.dot(xo_ref[...], wo_ref[DIFF_WIDTH + SSM_WIDTH:, :],
                        preferred_element_type=F32)
    h = x_ref[...] + acc
    h_out[...] = h
    hm_out[...] = _rms(h, gm_ref[...]).astype(BF16)


def _outproj(a_out, z, xo, x2d, w_glu, b_glu, g_ssm, w_out, g_mlp):
    rows, d = x2d.shape
    tm = ROW_TILE
    mix = w_out.shape[0]
    row = lambda i: (i, 0)
    const = lambda i: (0, 0)
    blk = (tm * DIFF_WIDTH * 2 + tm * SSM_WIDTH * 2 + tm * XATTN_WIDTH * 2 + tm * d * 4
           + SSM_WIDTH * SSM_WIDTH * 2 + mix * d * 2 + tm * d * 4 + tm * d * 2)
    return pl.pallas_call(
        _outproj_kernel,
        grid=(rows // tm,),
        in_specs=[
            pl.BlockSpec((tm, DIFF_WIDTH), row),
            pl.BlockSpec((SSM_TILES, tm, LANES), lambda i: (0, i, 0)),
            pl.BlockSpec((tm, XATTN_WIDTH), row),
            pl.BlockSpec((tm, d), row),
            pl.BlockSpec((SSM_WIDTH, SSM_WIDTH), const),
            pl.BlockSpec((1, SSM_WIDTH), const),
            pl.BlockSpec((1, SSM_WIDTH), const),
            pl.BlockSpec((mix, d), const),
            pl.BlockSpec((1, d), const),
        ],
        out_specs=[pl.BlockSpec((tm, d), row), pl.BlockSpec((tm, d), row)],
        out_shape=[jax.ShapeDtypeStruct((rows, d), F32), jax.ShapeDtypeStruct((rows, d), BF16)],
        compiler_params=pltpu.CompilerParams(
            dimension_semantics=("parallel",),
            vmem_limit_bytes=_vmem_limit(blk, 3 * tm * d * 4)),
        name="outproj",
    )(a_out, z, xo, x2d, w_glu, b_glu, g_ssm, w_out, g_mlp)


def _mlp_kernel(hm_ref, w1_ref, w2_ref, h_ref, o_ref):
    f = pl.program_id(1)

    @pl.when(f == 0)
    def _():
        o_ref[...] = h_ref[...]

    ff = jnp.dot(hm_ref[...], w1_ref[...], preferred_element_type=F32)
    ff = jnp.square(jnp.maximum(ff, 0.0)).astype(BF16)
    o_ref[...] += jnp.dot(ff, w2_ref[...], preferred_element_type=F32)


def _mlp(hm, h, w1, w2):
    rows, d = h.shape
    d_ff = w1.shape[1]
    tm, tf = ROW_TILE, MLP_FF_TILE
    blk = tm * d * 2 + d * tf * 2 + tf * d * 2 + tm * d * 4 + tm * d * 4
    return pl.pallas_call(
        _mlp_kernel,
        grid=(rows // tm, d_ff // tf),
        in_specs=[
            pl.BlockSpec((tm, d), lambda i, f: (i, 0)),
            pl.BlockSpec((d, tf), lambda i, f: (0, f)),
            pl.BlockSpec((tf, d), lambda i, f: (f, 0)),
            pl.BlockSpec((tm, d), lambda i, f: (i, 0)),
        ],
        out_specs=pl.BlockSpec((tm, d), lambda i, f: (i, 0)),
        out_shape=jax.ShapeDtypeStruct((rows, d), F32),
        compiler_params=pltpu.CompilerParams(
            dimension_semantics=("parallel", "arbitrary"),
            vmem_limit_bytes=_vmem_limit(blk, tm * tf * 6 + tm * d * 4)),
        name="mlp",
    )(hm, w1, w2, h)


def _rotary_tables(seq_len):
    dim = DIFF_QK_DIM
    inv_freq = 1.0 / (ROPE_THETA ** (jnp.arange(0, dim, 2, dtype=F32) / dim))
    freqs = jnp.arange(seq_len, dtype=F32)[:, None] * inv_freq[None, :]
    emb = jnp.concatenate([freqs, freqs, freqs, freqs], axis=-1)
    return jnp.cos(emb), jnp.sin(emb)


def kernel(x, mem, g_attn_norm, w_in, g_q, g_k, lam_q1, lam_k1, lam_q2, lam_k2, g_subln, lam_re, lam_im, log_dt, b_re, b_im, c_re, c_im, d_skip, w_glu, b_glu, g_ssm_out, g_mem, w_mem_kv, g_xq, g_xk, g_xattn_out, w_out, g_mlp_norm, w_mlp_in, w_mlp_out):
    b, s, d = x.shape
    m = mem.shape[1]
    depth = w_in.shape[0]
    rows = b * s
    cos2, sin2 = _rotary_tables(s)
    h = x.reshape(rows, d)
    mem2d = mem.reshape(b * m, d)
    row2 = lambda v: v.reshape(1, -1)
    for layer in range(depth):
        lambda_init = 0.8 - 0.6 * math.exp(-0.3 * layer)

        xk, xv = _memkv(mem2d, row2(g_mem[layer]), w_mem_kv[layer].astype(BF16),
                        row2(g_xk[layer]))
        q, k, vt, u, x_out = _inproj(
            h, s, row2(g_attn_norm[layer]), w_in[layer].astype(BF16),
            row2(jnp.tile(g_q[layer], 2)), row2(jnp.tile(g_k[layer], 2)), cos2, sin2,
            xk.reshape(b, m, XATTN_WIDTH), xv.reshape(b, m, XATTN_WIDTH),
            row2(g_xq[layer]), row2(g_xattn_out[layer]))

        a_out = _attention(q.reshape(b, s, DIFF_WIDTH), k.reshape(b, s, DIFF_WIDTH),
                           vt, row2(lam_q1[layer]), row2(lam_k1[layer]),
                           row2(lam_q2[layer]), row2(lam_k2[layer]),
                           g_subln[layer].reshape(DIFF_V_DIM, 1), lambda_init)

        m_op, w_op, n_op, a_pow = _ssm_prep(lam_re[layer], lam_im[layer], log_dt[layer],
                                            b_re[layer], b_im[layer], c_re[layer], c_im[layer],
                                            d_skip[layer])
        u2 = u.reshape(SSM_TILES, rows // SSM_CHUNK, SSM_CHUNK * LANES)
        x_state = _ssm_scan(_ssm_sum(u2, w_op), a_pow, b)
        z = _ssm_out(u2, m_op, x_state, n_op).reshape(SSM_TILES, rows, LANES)

        h, hm = _outproj(a_out.reshape(rows, DIFF_WIDTH), z, x_out, h,
                         w_glu[layer].astype(BF16), row2(b_glu[layer]), row2(g_ssm_out[layer]),
                         w_out[layer].astype(BF16), row2(g_mlp_norm[layer]))
        h = _mlp(hm, h, w_mlp_in[layer].astype(BF16), w_mlp_out[layer].astype(BF16))
    return h.reshape(b, s, d)
```

```python
import functools
import math

import jax
import jax.numpy as jnp
from jax import lax
from jax.experimental import pallas as pl
from jax.experimental.pallas import tpu as pltpu

F32 = jnp.float32
BF16 = jnp.bfloat16

LANES = 128
VMEM_CAP_BYTES = 60000 * 1024

DIFF_HEADS = 8
DIFF_QK_DIM = 64
DIFF_V_DIM = 2 * DIFF_QK_DIM
DIFF_WIDTH = DIFF_HEADS * DIFF_V_DIM
SSM_GROUPS = 32
SSM_GROUP_CH = 16
SSM_STATE = 64
SSM_WIDTH = SSM_GROUPS * SSM_GROUP_CH
XATTN_HEADS = 4
XATTN_HEAD_DIM = 128
XATTN_WIDTH = XATTN_HEADS * XATTN_HEAD_DIM
ROPE_THETA = 10000.0
EPS = 1e-6
LOG2_E = math.log2(math.e)

SSM_CHUNK = 16
SSM_LANE_GROUPS = LANES // SSM_GROUP_CH
SSM_TILES = SSM_WIDTH // LANES
SSM_TILE_STATE = SSM_LANE_GROUPS * SSM_STATE

ROW_TILE = 512
ATTN_TILE = 512
ATTN_KEY_TILE = 256
ATTN_ONES_ROWS = 16
MLP_FF_TILE = 1024

_NT = (((1,), (1,)), ((), ()))


def _vmem_limit(block_bytes, scratch_bytes=0):
    return int(min(2 * block_bytes + scratch_bytes + (4 << 20), VMEM_CAP_BYTES))


def _rms(t, g):
    ms = jnp.mean(t * t, axis=-1, keepdims=True)
    return t * lax.rsqrt(ms + EPS) * g


def _memkv_kernel(mem_ref, g_ref, w_ref, gxk_ref, k_out, v_out):
    hn = _rms(mem_ref[...], g_ref[...]).astype(BF16)
    kv = jnp.dot(hn, w_ref[...], preferred_element_type=F32)
    gxk = gxk_ref[...]
    for hh in range(XATTN_HEADS):
        sl = slice(hh * XATTN_HEAD_DIM, (hh + 1) * XATTN_HEAD_DIM)
        k_out[:, sl] = _rms(kv[:, sl], gxk).astype(BF16)
    v_out[...] = kv[:, XATTN_WIDTH:].astype(BF16)


def _memkv(mem2d, g_mem, w_kv, g_xk):
    rows, d = mem2d.shape
    blk = rows * d * 4 + d * 2 * XATTN_WIDTH * 2 + 2 * rows * XATTN_WIDTH * 2
    return pl.pallas_call(
        _memkv_kernel,
        grid=(1,),
        in_specs=[
            pl.BlockSpec((rows, d), lambda i: (0, 0)),
            pl.BlockSpec((1, d), lambda i: (0, 0)),
            pl.BlockSpec((d, 2 * XATTN_WIDTH), lambda i: (0, 0)),
            pl.BlockSpec((1, XATTN_HEAD_DIM), lambda i: (0, 0)),
        ],
        out_specs=[
            pl.BlockSpec((rows, XATTN_WIDTH), lambda i: (0, 0)),
            pl.BlockSpec((rows, XATTN_WIDTH), lambda i: (0, 0)),
        ],
        out_shape=[jax.ShapeDtypeStruct((rows, XATTN_WIDTH), BF16)] * 2,
        compiler_params=pltpu.CompilerParams(
            vmem_limit_bytes=_vmem_limit(blk, rows * 2 * XATTN_WIDTH * 8)),
        name="memkv",
    )(mem2d, g_mem, w_kv, g_xk)


def _inproj_kernel(x_ref, gn_ref, w_ref, gq_ref, gk_ref, cos_ref, sin_ref,
                   xk_ref, xv_ref, gxq_ref, gxo_ref,
                   q_out, k_out, v_out, u_out, xo_out):
    hn = _rms(x_ref[...], gn_ref[...]).astype(BF16)
    tm = hn.shape[0]
    tn = DIFF_WIDTH

    def project(j):
        return jnp.dot(hn, w_ref[:, j * tn:(j + 1) * tn], preferred_element_type=F32)

    def qk_epilogue(acc, g_ref, out_ref, scale):
        lane = lax.broadcasted_iota(jnp.int32, (tm, LANES), 1)
        lo_comp = lane < DIFF_QK_DIM
        first_half = (lane & (DIFF_QK_DIM - 1)) < DIFF_QK_DIM // 2
        cos = cos_ref[...]
        sin = sin_ref[...]
        g = g_ref[...]
        for h in range(DIFF_HEADS):
            sl = slice(h * LANES, (h + 1) * LANES)
            t = acc[:, sl]
            t2 = t * t
            s_lo = jnp.sum(jnp.where(lo_comp, t2, 0.0), axis=-1, keepdims=True)
            s_hi = jnp.sum(jnp.where(lo_comp, 0.0, t2), axis=-1, keepdims=True)
            ms = jnp.where(lo_comp, s_lo, s_hi) * (1.0 / DIFF_QK_DIM)
            tn_ = t * lax.rsqrt(ms + EPS) * g
            fwd = pltpu.roll(tn_, DIFF_QK_DIM // 2, 1)
            bwd = pltpu.roll(tn_, LANES - DIFF_QK_DIM // 2, 1)
            rot = jnp.where(first_half, -bwd, fwd)
            out_ref[:, sl] = ((tn_ * cos + rot * sin) * scale).astype(BF16)

    qk_epilogue(project(0), gq_ref, q_out, DIFF_QK_DIM ** -0.5 * LOG2_E)
    qk_epilogue(project(1), gk_ref, k_out, 1.0)
    v_out[0] = project(2).T.astype(BF16)

    acc = project(3)
    for c in range(SSM_TILES):
        u_out[c] = acc[:, c * LANES:(c + 1) * LANES].astype(BF16)
    gxq = gxq_ref[...]
    outs = []
    ssq = jnp.zeros((tm, 1), F32)
    for hh in range(XATTN_HEADS):
        sl = slice(hh * XATTN_HEAD_DIM, (hh + 1) * XATTN_HEAD_DIM)
        t = acc[:, SSM_WIDTH + hh * XATTN_HEAD_DIM:SSM_WIDTH + (hh + 1) * XATTN_HEAD_DIM]
        qn = (_rms(t, gxq) * (XATTN_HEAD_DIM ** -0.5)).astype(BF16)
        s = lax.dot_general(qn, xk_ref[0, :, sl], _NT, preferred_element_type=F32)
        p = jnp.exp(s - jnp.max(s, axis=-1, keepdims=True))
        l = jnp.sum(p, axis=-1, keepdims=True)
        o = jnp.dot(p.astype(BF16), xv_ref[0, :, sl], preferred_element_type=F32) / l
        outs.append(o)
        ssq = ssq + jnp.sum(o * o, axis=-1, keepdims=True)
    inv = lax.rsqrt(ssq * (1.0 / XATTN_WIDTH) + EPS)
    for hh in range(XATTN_HEADS):
        sl = slice(hh * XATTN_HEAD_DIM, (hh + 1) * XATTN_HEAD_DIM)
        xo_out[:, sl] = (outs[hh] * inv * gxo_ref[:, sl]).astype(BF16)


def _inproj(x2d, seq, g_norm, w_in, g_q2, g_k2, cos2, sin2, xk, xv, g_xq, g_xo):
    rows, d = x2d.shape
    tm = ROW_TILE
    tn = DIFF_WIDTH
    n_in = w_in.shape[1]
    tiles_per_seq = seq // tm
    mem_len = xk.shape[1]
    blk = (tm * d * 4 + 2 * tm * LANES * 4 + 2 * mem_len * XATTN_WIDTH * 2
           + 3 * tm * tn * 2 + tm * SSM_WIDTH * 2 + tm * XATTN_WIDTH * 2)
    row = lambda i: (i, 0)
    const = lambda i: (0, 0)
    return pl.pallas_call(
        _inproj_kernel,
        grid=(rows // tm,),
        in_specs=[
            pl.BlockSpec((tm, d), row),
            pl.BlockSpec((1, d), const),
            pl.BlockSpec((d, n_in), const, pipeline_mode=pl.Buffered(1)),
            pl.BlockSpec((1, LANES), const),
            pl.BlockSpec((1, LANES), const),
            pl.BlockSpec((tm, LANES), lambda i: (i % tiles_per_seq, 0)),
            pl.BlockSpec((tm, LANES), lambda i: (i % tiles_per_seq, 0)),
            pl.BlockSpec((1, mem_len, XATTN_WIDTH), lambda i: (i // tiles_per_seq, 0, 0)),
            pl.BlockSpec((1, mem_len, XATTN_WIDTH), lambda i: (i // tiles_per_seq, 0, 0)),
            pl.BlockSpec((1, XATTN_HEAD_DIM), const),
            pl.BlockSpec((1, XATTN_WIDTH), const),
        ],
        out_specs=[
            pl.BlockSpec((tm, tn), row),
            pl.BlockSpec((tm, tn), row),
            pl.BlockSpec((1, tn, tm), lambda i: (i // tiles_per_seq, 0, i % tiles_per_seq)),
            pl.BlockSpec((SSM_TILES, tm, LANES), lambda i: (0, i, 0)),
            pl.BlockSpec((tm, XATTN_WIDTH), row),
        ],
        out_shape=[
            jax.ShapeDtypeStruct((rows, tn), BF16),
            jax.ShapeDtypeStruct((rows, tn), BF16),
            jax.ShapeDtypeStruct((rows // seq, tn, seq), BF16),
            jax.ShapeDtypeStruct((SSM_TILES, rows, LANES), BF16),
            jax.ShapeDtypeStruct((rows, XATTN_WIDTH), BF16),
        ],
        compiler_params=pltpu.CompilerParams(
            dimension_semantics=("parallel",),
            vmem_limit_bytes=_vmem_limit(blk, d * n_in * 2 + tm * d * 2 + 6 * tm * tn * 4)),
        name="inproj",
    )(x2d, g_norm, w_in, g_q2, g_k2, cos2, sin2, xk, xv, g_xq, g_xo)


def _attn_kernel(lq1_ref, lk1_ref, lq2_ref, lk2_ref, gsub_ref, q_ref, k_ref, vt_ref, o_ref,
                 s_scr, acc_scr, *, lambda_init):
    qi = pl.program_id(2)
    tq = q_ref.shape[1]
    tk = ATTN_KEY_TILE
    blocks_per_q = tq // tk
    q = q_ref[0]
    lane = lax.broadcasted_iota(jnp.int32, (tq, LANES), 1)
    zero = jnp.zeros_like(q)
    qs = (jnp.where(lane < DIFF_QK_DIM, q, zero), jnp.where(lane >= DIFF_QK_DIM, q, zero))

    acc_scr[...] = jnp.zeros(acc_scr.shape, F32)

    def scores(kb, slot):
        k = k_ref[0, pl.ds(pl.multiple_of(kb * tk, tk), tk), :]
        for c in range(2):
            s_scr[slot, c] = lax.dot_general(k, qs[c], _NT, preferred_element_type=F32)

    def softmax_pv(kb, slot, stats, diag_index=None):
        vt = vt_ref[0, :, pl.ds(pl.multiple_of(kb * tk, tk), tk)]
        vt_ones = jnp.concatenate([vt, jnp.ones((ATTN_ONES_ROWS, tk), BF16)], axis=0)
        new_stats = []
        for c in range(2):
            m_old = stats[c]
            s = s_scr[slot, c]
            if diag_index is not None:
                key = lax.broadcasted_iota(jnp.int32, (tk, tq), 0) + diag_index * tk
                qry = lax.broadcasted_iota(jnp.int32, (tk, tq), 1)
                s = jnp.where(key <= qry, s, -jnp.inf)
            m_new = jnp.maximum(m_old, jnp.max(s, axis=0, keepdims=True))
            alpha = jnp.exp2(m_old - m_new)
            p = jnp.exp2((s - m_new).astype(BF16))
            acc_scr[c] = alpha * acc_scr[c] + jnp.dot(vt_ones, p, preferred_element_type=F32)
            new_stats.append(m_new)
        return tuple(new_stats)

    def pair(i, stats):
        kb = blocks_per_q * i
        for j in range(blocks_per_q):
            scores(kb + j + 1, (j + 1) % 2)
            stats = softmax_pv(kb + j, j % 2, stats)
        return stats

    init = (jnp.full((1, tq), -jnp.inf, F32),) * 2
    def double_pair(i2, stats):
        return pair(2 * i2 + 1, pair(2 * i2, stats))

    scores(0, 0)
    stats = lax.fori_loop(0, qi // 2, double_pair, init)
    stats = lax.cond(qi % 2 == 1, lambda st: pair(qi - 1, st), lambda st: st, stats)
    kb = blocks_per_q * qi
    for j in range(blocks_per_q):
        if j + 1 < blocks_per_q:
            scores(kb + j + 1, (j + 1) % 2)
        stats = softmax_pv(kb + j, j % 2, stats, diag_index=j)

    lam = (jnp.exp(jnp.sum(lq1_ref[...] * lk1_ref[...], axis=-1, keepdims=True))
           - jnp.exp(jnp.sum(lq2_ref[...] * lk2_ref[...], axis=-1, keepdims=True))
           + lambda_init)
    nv = DIFF_V_DIM
    o_t = (acc_scr[0, 0:nv] / acc_scr[0, nv:nv + 1]
           - lam * (acc_scr[1, 0:nv] / acc_scr[1, nv:nv + 1]))
    ms = jnp.mean(o_t * o_t, axis=0, keepdims=True)
    o_t = o_t * lax.rsqrt(ms + EPS) * (gsub_ref[...] * (1.0 - lambda_init))
    o_ref[0] = o_t.T.astype(BF16)


def _attention(q, k, vt, lq1, lk1, lq2, lk2, g_subln_col, lambda_init):
    b, s, _ = q.shape
    tq = ATTN_TILE
    tk = ATTN_KEY_TILE
    assert tq % (2 * tk) == 0
    vec = pl.BlockSpec((1, DIFF_QK_DIM), lambda bi, h, qi: (0, 0))
    blk = 2 * tq * LANES * 2 + 2 * s * LANES * 2
    return pl.pallas_call(
        functools.partial(_attn_kernel, lambda_init=lambda_init),
        grid=(b, DIFF_HEADS, s // tq),
        in_specs=[
            vec, vec, vec, vec,
            pl.BlockSpec((DIFF_V_DIM, 1), lambda bi, h, qi: (0, 0)),
            pl.BlockSpec((1, tq, LANES), lambda bi, h, qi: (bi, qi, h)),
            pl.BlockSpec((1, s, LANES), lambda bi, h, qi: (bi, 0, h)),
            pl.BlockSpec((1, DIFF_V_DIM, s), lambda bi, h, qi: (bi, h, 0)),
        ],
        out_specs=pl.BlockSpec((1, tq, LANES), lambda bi, h, qi: (bi, qi, h)),
        out_shape=jax.ShapeDtypeStruct((b, s, DIFF_WIDTH), BF16),
        scratch_shapes=[
            pltpu.VMEM((2, 2, tk, tq), F32),
            pltpu.VMEM((2, DIFF_V_DIM + ATTN_ONES_ROWS, tq), F32),
        ],
        compiler_params=pltpu.CompilerParams(
            dimension_semantics=("parallel", "parallel", "arbitrary"),
            vmem_limit_bytes=_vmem_limit(blk, 2 * DIFF_V_DIM * tq * 4 + 12 * tk * tq * 4)),
        name="diffattn",
    )(lq1, lk1, lq2, lk2, g_subln_col, q, k, vt)


def _ssm_prep_kernel(lr_row, li_row, ld_row, lr_col, li_col, ld_col,
                     bre_ref, bim_ref, cre_ref, cim_ref, d_ref,
                     m_out, w_out, n_out, a_out):
    t_len = SSM_CHUNK
    ns = SSM_TILE_STATE

    def lam_bar(lr, li, ld):
        dt = jnp.exp(ld)
        e = jnp.exp(lr * dt)
        return e * jnp.cos(li * dt), e * jnp.sin(li * dt)

    lam_r = lr_row[0]
    lam_i = li_row[0]
    br, bi = lam_bar(lam_r, lam_i, ld_row[0])
    den = lam_r * lam_r + lam_i * lam_i
    nr = br - 1.0
    coef_r = (nr * lam_r + bi * lam_i) / den
    coef_i = (bi * lam_r - nr * lam_i) / den
    b_r = bre_ref[0]
    b_i = bim_ref[0]
    bb_r = coef_r * b_r - coef_i * b_i
    bb_i = coef_r * b_i + coef_i * b_r
    pr = jnp.ones_like(br)
    pi = jnp.zeros_like(br)
    for tau in range(t_len):
        rows = slice((t_len - 1 - tau) * LANES, (t_len - tau) * LANES)
        w_out[0, rows, 0:ns] = (bb_r * pr - bb_i * pi).astype(BF16)
        w_out[0, rows, ns:2 * ns] = (bb_r * pi + bb_i * pr).astype(BF16)
        pr, pi = pr * br - pi * bi, pr * bi + pi * br
    a_out[0, :, 0:ns] = pr
    a_out[0, :, ns:2 * ns] = pi

    cr, ci = lam_bar(lr_col[0], li_col[0], ld_col[0])
    c_r = cre_ref[0]
    c_i = cim_ref[0]
    qr = jnp.ones_like(cr)
    qi = jnp.zeros_like(cr)
    eye = (lax.broadcasted_iota(jnp.int32, (LANES, LANES), 0)
           == lax.broadcasted_iota(jnp.int32, (LANES, LANES), 1))
    kblk = []
    for tau in range(t_len + 1):
        cl_r = c_r * qr - c_i * qi
        cl_i = c_r * qi + c_i * qr
        if tau < t_len:
            kb = (jnp.dot(bb_r, cl_r, precision=lax.Precision.HIGHEST, preferred_element_type=F32)
                  - jnp.dot(bb_i, cl_i, precision=lax.Precision.HIGHEST, preferred_element_type=F32))
            if tau == 0:
                kb = kb + jnp.where(eye, d_ref[0], 0.0)
            kblk.append(kb.astype(BF16))
        if tau >= 1:
            cols = slice((tau - 1) * LANES, tau * LANES)
            n_out[0, 0:ns, cols] = cl_r.astype(BF16)
            n_out[0, ns:2 * ns, cols] = (-cl_i).astype(BF16)
        qr, qi = qr * cr - qi * ci, qr * ci + qi * cr
    zeros = jnp.zeros((LANES, LANES), BF16)
    for s in range(t_len):
        for t in range(t_len):
            m_out[0, s * LANES:(s + 1) * LANES, t * LANES:(t + 1) * LANES] = (
                kblk[t - s] if t >= s else zeros)


def _ssm_prep(lam_re, lam_im, log_dt, b_re, b_im, c_re, c_im, d_skip):
    nj, gj, p, h = SSM_TILES, SSM_LANE_GROUPS, SSM_STATE, SSM_GROUP_CH
    ns, tl = SSM_TILE_STATE, SSM_CHUNK * LANES
    eye = jnp.eye(gj, dtype=F32)
    dt_full = jnp.broadcast_to(log_dt[:, None], (SSM_GROUPS, p))

    def expand_b(b):
        b4 = b.reshape(nj, gj, p, h).transpose(0, 1, 3, 2)
        return (b4[:, :, :, None, :] * eye[None, :, None, :, None]).reshape(nj, LANES, ns)

    def expand_c(c):
        c4 = c.reshape(nj, gj, h, p).transpose(0, 1, 3, 2)
        return (c4[:, :, :, None, :] * eye[None, :, None, :, None]).reshape(nj, ns, LANES)

    args = (lam_re.reshape(nj, 1, ns), lam_im.reshape(nj, 1, ns), dt_full.reshape(nj, 1, ns),
            lam_re.reshape(nj, ns, 1), lam_im.reshape(nj, ns, 1), dt_full.reshape(nj, ns, 1),
            expand_b(b_re), expand_b(b_im), expand_c(c_re), expand_c(c_im),
            d_skip.reshape(nj, 1, LANES))
    spec = lambda shp: pl.BlockSpec((1,) + shp, lambda j: (j, 0, 0))
    out_bytes = tl * tl * 2 + 2 * tl * 2 * ns * 2 + 2 * ns * 4
    return pl.pallas_call(
        _ssm_prep_kernel,
        grid=(nj,),
        in_specs=[spec((1, ns))] * 3 + [spec((ns, 1))] * 3 + [spec((LANES, ns))] * 2
                 + [spec((ns, LANES))] * 2 + [spec((1, LANES))],
        out_specs=[spec((tl, tl)), spec((tl, 2 * ns)), spec((2 * ns, tl)), spec((1, 2 * ns))],
        out_shape=[
            jax.ShapeDtypeStruct((nj, tl, tl), BF16),
            jax.ShapeDtypeStruct((nj, tl, 2 * ns), BF16),
            jax.ShapeDtypeStruct((nj, 2 * ns, tl), BF16),
            jax.ShapeDtypeStruct((nj, 1, 2 * ns), F32),
        ],
        compiler_params=pltpu.CompilerParams(
            dimension_semantics=("parallel",),
            vmem_limit_bytes=_vmem_limit(out_bytes + 7 * LANES * ns * 4, 8 << 20)),
        name="ssm_prep",
    )(*args)


def _ssm_sum_kernel(u_ref, w_ref, s_out):
    s_out[0] = jnp.dot(u_ref[0], w_ref[0], preferred_element_type=F32)


def _ssm_sum(u2, w_op):
    nj, rows, tl = u2.shape
    ns2 = w_op.shape[2]
    spec = lambda shp: pl.BlockSpec((1,) + shp, lambda j: (j, 0, 0))
    blk = rows * tl * 2 + tl * ns2 * 2 + rows * ns2 * 4
    return pl.pallas_call(
        _ssm_sum_kernel,
        grid=(nj,),
        in_specs=[spec((rows, tl)), spec((tl, ns2))],
        out_specs=spec((rows, ns2)),
        out_shape=jax.ShapeDtypeStruct((nj, rows, ns2), F32),
        compiler_params=pltpu.CompilerParams(
            dimension_semantics=("parallel",), vmem_limit_bytes=_vmem_limit(blk)),
        name="ssm_sum",
    )(u2, w_op)


def _ssm_scan_kernel(s_ref, a_ref, x_out, *, n_batch, n_chunk):
    ns = SSM_TILE_STATE
    a = a_ref[0]
    ar = a[:, 0:ns]
    ai = a[:, ns:2 * ns]

    def body(c, carry):
        new = []
        for b in range(n_batch):
            xr, xi = carry[2 * b], carry[2 * b + 1]
            r = b * n_chunk + c
            x_out[0, pl.ds(r, 1), 0:ns] = xr
            x_out[0, pl.ds(r, 1), ns:2 * ns] = xi
            s = s_ref[0, pl.ds(r, 1), :]
            new.append(ar * xr - ai * xi + s[:, 0:ns])
            new.append(ar * xi + ai * xr + s[:, ns:2 * ns])
        return tuple(new)

    init = tuple(jnp.zeros((1, ns), F32) for _ in range(2 * n_batch))
    lax.fori_loop(0, n_chunk, body, init)


def _ssm_scan(s_sum, a_pow, n_batch):
    nj, rows, ns2 = s_sum.shape
    spec = lambda shp: pl.BlockSpec((1,) + shp, lambda j: (j, 0, 0))
    return pl.pallas_call(
        functools.partial(_ssm_scan_kernel, n_batch=n_batch, n_chunk=rows // n_batch),
        grid=(nj,),
        in_specs=[spec((rows, ns2)), spec((1, ns2))],
        out_specs=spec((rows, ns2)),
        out_shape=jax.ShapeDtypeStruct((nj, rows, ns2), F32),
        compiler_params=pltpu.CompilerParams(
            dimension_semantics=("parallel",),
            vmem_limit_bytes=_vmem_limit(2 * rows * ns2 * 4)),
        name="ssm_scan",
    )(s_sum, a_pow)


def _gelu_tanh(x):
    return x * (0.5 * (1.0 + jnp.tanh(math.sqrt(2.0 / math.pi) * (x + 0.044715 * (x * x * x)))))


def _ssm_out_kernel(u_ref, m_ref, x_ref, n_ref, z_out):
    y = jnp.dot(u_ref[0], m_ref[0], preferred_element_type=F32)
    y = y + jnp.dot(x_ref[0].astype(BF16), n_ref[0], preferred_element_type=F32)
    z_out[0] = _gelu_tanh(y).astype(BF16)


def _ssm_out(u2, m_op, x_state, n_op):
    nj, rows, tl = u2.shape
    ns2 = x_state.shape[2]
    n_split = 2
    tr = rows // n_split
    rspec = lambda shp: pl.BlockSpec((1,) + shp, lambda j, r: (j, r, 0))
    cspec = lambda shp: pl.BlockSpec((1,) + shp, lambda j, r: (j, 0, 0))
    blk = tr * tl * 2 + tl * tl * 2 + tr * ns2 * 4 + ns2 * tl * 2 + tr * tl * 2
    return pl.pallas_call(
        _ssm_out_kernel,
        grid=(nj, n_split),
        in_specs=[rspec((tr, tl)), cspec((tl, tl)), rspec((tr, ns2)), cspec((ns2, tl))],
        out_specs=rspec((tr, tl)),
        out_shape=jax.ShapeDtypeStruct((nj, rows, tl), BF16),
        compiler_params=pltpu.CompilerParams(
            dimension_semantics=("parallel", "arbitrary"),
            vmem_limit_bytes=_vmem_limit(blk, 3 * tr * tl * 4)),
        name="ssm_out",
    )(u2, m_op, x_state, n_op)


def _outproj_kernel(a_ref, z_ref, xo_ref, x_ref, wglu_ref, bglu_ref, gs_ref, wo_ref, gm_ref,
                    h_out, hm_out):
    half = x_ref.shape[0] // 2
    halves = [slice(r * half, (r + 1) * half) for r in range(2)]
    zs = [jnp.concatenate([z_ref[c, rows, :] for c in range(SSM_TILES)], axis=-1)
          for rows in halves]
    gates = [jnp.dot(z, wglu_ref[...], preferred_element_type=F32) + bglu_ref[...] for z in zs]
    for rows, z, gate_in in zip(halves, zs, gates):
        acc = jnp.dot(a_ref[rows, :], wo_ref[0:DIFF_WIDTH, :], preferred_element_type=F32)
        acc = acc + jnp.dot(xo_ref[rows, :], wo_ref[DIFF_WIDTH + SSM_WIDTH:, :],
                            preferred_element_type=F32)
        zg = z.astype(F32) * (1.0 / (1.0 + jnp.exp(-gate_in)))
        s_out = _rms(zg, gs_ref[...]).astype(BF16)
        acc = acc + jnp.dot(s_out, wo_ref[DIFF_WIDTH:DIFF_WIDTH + SSM_WIDTH, :],
                            preferred_element_type=F32)
        h = x_ref[rows, :] + acc
        h_out[rows, :] = h
        hm_out[rows, :] = _rms(h, gm_ref[...]).astype(BF16)


def _outproj(a_out, z, xo, x2d, w_glu, b_glu, g_ssm, w_out, g_mlp):
    rows, d = x2d.shape
    tm = ROW_TILE
    mix = w_out.shape[0]
    row = lambda i: (i, 0)
    const = lambda i: (0, 0)
    blk = (tm * DIFF_WIDTH * 2 + tm * SSM_WIDTH * 2 + tm * XATTN_WIDTH * 2 + tm * d * 4
           + SSM_WIDTH * SSM_WIDTH * 2 + mix * d * 2 + tm * d * 4 + tm * d * 2)
    return pl.pallas_call(
        _outproj_kernel,
        grid=(rows // tm,),
        in_specs=[
            pl.BlockSpec((tm, DIFF_WIDTH), row),
            pl.BlockSpec((SSM_TILES, tm, LANES), lambda i: (0, i, 0)),
            pl.BlockSpec((tm, XATTN_WIDTH), row),
            pl.BlockSpec((tm, d), row),
            pl.BlockSpec((SSM_WIDTH, SSM_WIDTH), const),
            pl.BlockSpec((1, SSM_WIDTH), const),
            pl.BlockSpec((1, SSM_WIDTH), const),
            pl.BlockSpec((mix, d), const),
            pl.BlockSpec((1, d), const),
        ],
        out_specs=[pl.BlockSpec((tm, d), row), pl.BlockSpec((tm, d), row)],
        out_shape=[jax.ShapeDtypeStruct((rows, d), F32), jax.ShapeDtypeStruct((rows, d), BF16)],
        compiler_params=pltpu.CompilerParams(
            dimension_semantics=("parallel",),
            vmem_limit_bytes=_vmem_limit(blk, 3 * tm * d * 4)),
        name="outproj",
    )(a_out, z, xo, x2d, w_glu, b_glu, g_ssm, w_out, g_mlp)


def _mlp_kernel(hm_ref, w1_ref, w2_ref, h_ref, o_ref):
    f = pl.program_id(1)

    @pl.when(f == 0)
    def _():
        o_ref[...] = h_ref[...]

    half = hm_ref.shape[0] // 2
    halves = [slice(r * half, (r + 1) * half) for r in range(2)]
    ffs = [jnp.dot(hm_ref[rows, :], w1_ref[...], preferred_element_type=F32) for rows in halves]
    for rows, ff in zip(halves, ffs):
        act = jnp.square(jnp.maximum(ff, 0.0)).astype(BF16)
        o_ref[rows, :] += jnp.dot(act, w2_ref[...], preferred_element_type=F32)


def _mlp(hm, h, w1, w2):
    rows, d = h.shape
    d_ff = w1.shape[1]
    tm, tf = ROW_TILE, MLP_FF_TILE
    blk = tm * d * 2 + d * tf * 2 + tf * d * 2 + tm * d * 4 + tm * d * 4
    return pl.pallas_call(
        _mlp_kernel,
        grid=(rows // tm, d_ff // tf),
        in_specs=[
            pl.BlockSpec((tm, d), lambda i, f: (i, 0)),
            pl.BlockSpec((d, tf), lambda i, f: (0, f)),
            pl.BlockSpec((tf, d), lambda i, f: (f, 0)),
            pl.BlockSpec((tm, d), lambda i, f: (i, 0)),
        ],
        out_specs=pl.BlockSpec((tm, d), lambda i, f: (i, 0)),
        out_shape=jax.ShapeDtypeStruct((rows, d), F32),
        compiler_params=pltpu.CompilerParams(
            dimension_semantics=("parallel", "arbitrary"),
            vmem_limit_bytes=_vmem_limit(blk, tm * tf * 6 + tm * d * 4)),
        name="mlp",
    )(hm, w1, w2, h)


def _rotary_tables(seq_len):
    dim = DIFF_QK_DIM
    inv_freq = 1.0 / (ROPE_THETA ** (jnp.arange(0, dim, 2, dtype=F32) / dim))
    freqs = jnp.arange(seq_len, dtype=F32)[:, None] * inv_freq[None, :]
    emb = jnp.concatenate([freqs, freqs, freqs, freqs], axis=-1)
    return jnp.cos(emb), jnp.sin(emb)


def kernel(x, mem, g_attn_norm, w_in, g_q, g_k, lam_q1, lam_k1, lam_q2, lam_k2, g_subln, lam_re, lam_im, log_dt, b_re, b_im, c_re, c_im, d_skip, w_glu, b_glu, g_ssm_out, g_mem, w_mem_kv, g_xq, g_xk, g_xattn_out, w_out, g_mlp_norm, w_mlp_in, w_mlp_out):
    b, s, d = x.shape
    m = mem.shape[1]
    depth = w_in.shape[0]
    rows = b * s
    cos2, sin2 = _rotary_tables(s)
    h = x.reshape(rows, d)
    mem2d = mem.reshape(b * m, d)
    row2 = lambda v: v.reshape(1, -1)
    for layer in range(depth):
        lambda_init = 0.8 - 0.6 * math.exp(-0.3 * layer)

        xk, xv = _memkv(mem2d, row2(g_mem[layer]), w_mem_kv[layer].astype(BF16),
                        row2(g_xk[layer]))
        q, k, vt, u, x_out = _inproj(
            h, s, row2(g_attn_norm[layer]), w_in[layer].astype(BF16),
            row2(jnp.tile(g_q[layer], 2)), row2(jnp.tile(g_k[layer], 2)), cos2, sin2,
            xk.reshape(b, m, XATTN_WIDTH), xv.reshape(b, m, XATTN_WIDTH),
            row2(g_xq[layer]), row2(g_xattn_out[layer]))

        a_out = _attention(q.reshape(b, s, DIFF_WIDTH), k.reshape(b, s, DIFF_WIDTH),
                           vt, row2(lam_q1[layer]), row2(lam_k1[layer]),
                           row2(lam_q2[layer]), row2(lam_k2[layer]),
                           g_subln[layer].reshape(DIFF_V_DIM, 1), lambda_init)

        m_op, w_op, n_op, a_pow = _ssm_prep(lam_re[layer], lam_im[layer], log_dt[layer],
                                            b_re[layer], b_im[layer], c_re[layer], c_im[layer],
                                            d_skip[layer])
        u2 = u.reshape(SSM_TILES, rows // SSM_CHUNK, SSM_CHUNK * LANES)
        x_state = _ssm_scan(_ssm_sum(u2, w_op), a_pow, b)
        z = _ssm_out(u2, m_op, x_state, n_op).reshape(SSM_TILES, rows, LANES)

        h, hm = _outproj(a_out.reshape(rows, DIFF_WIDTH), z, x_out, h,
                         w_glu[layer].astype(BF16), row2(b_glu[layer]), row2(g_ssm_out[layer]),
                         w_out[layer].astype(BF16), row2(g_mlp_norm[layer]))
        h = _mlp(hm, h, w_mlp_in[layer].astype(BF16), w_mlp_out[layer].astype(BF16))
    return h.reshape(b, s, d)
```

```python
import functools
import math

import jax
import jax.numpy as jnp
from jax import lax
from jax.experimental import pallas as pl
from jax.experimental.pallas import tpu as pltpu

F32 = jnp.float32
BF16 = jnp.bfloat16

LANES = 128
VMEM_CAP_BYTES = 60000 * 1024

DIFF_HEADS = 8
DIFF_QK_DIM = 64
DIFF_V_DIM = 2 * DIFF_QK_DIM
DIFF_WIDTH = DIFF_HEADS * DIFF_V_DIM
SSM_GROUPS = 32
SSM_GROUP_CH = 16
SSM_STATE = 64
SSM_WIDTH = SSM_GROUPS * SSM_GROUP_CH
XATTN_HEADS = 4
XATTN_HEAD_DIM = 128
XATTN_WIDTH = XATTN_HEADS * XATTN_HEAD_DIM
ROPE_THETA = 10000.0
EPS = 1e-6
LOG2_E = math.log2(math.e)

SSM_CHUNK = 16
SSM_LANE_GROUPS = LANES // SSM_GROUP_CH
SSM_TILES = SSM_WIDTH // LANES
SSM_TILE_STATE = SSM_LANE_GROUPS * SSM_STATE

ROW_TILE = 512
ATTN_TILE = 512
ATTN_KEY_TILE = 256
ATTN_ONES_ROWS = 16
MLP_FF_TILE = 1024

_NT = (((1,), (1,)), ((), ()))


def _vmem_limit(block_bytes, scratch_bytes=0):
    return int(min(2 * block_bytes + scratch_bytes + (4 << 20), VMEM_CAP_BYTES))


def _rms(t, g):
    ms = jnp.mean(t * t, axis=-1, keepdims=True)
    return t * lax.rsqrt(ms + EPS) * g


def _memkv_kernel(mem_ref, g_ref, w_ref, gxk_ref, k_out, v_out):
    hn = _rms(mem_ref[...], g_ref[...]).astype(BF16)
    kv = jnp.dot(hn, w_ref[...], preferred_element_type=F32)
    gxk = gxk_ref[...]
    for hh in range(XATTN_HEADS):
        sl = slice(hh * XATTN_HEAD_DIM, (hh + 1) * XATTN_HEAD_DIM)
        k_out[:, sl] = _rms(kv[:, sl], gxk).astype(BF16)
    v_out[...] = kv[:, XATTN_WIDTH:].astype(BF16)


def _memkv(mem2d, g_mem, w_kv, g_xk):
    rows, d = mem2d.shape
    blk = rows * d * 4 + d * 2 * XATTN_WIDTH * 2 + 2 * rows * XATTN_WIDTH * 2
    return pl.pallas_call(
        _memkv_kernel,
        grid=(1,),
        in_specs=[
            pl.BlockSpec((rows, d), lambda i: (0, 0)),
            pl.BlockSpec((1, d), lambda i: (0, 0)),
            pl.BlockSpec((d, 2 * XATTN_WIDTH), lambda i: (0, 0)),
            pl.BlockSpec((1, XATTN_HEAD_DIM), lambda i: (0, 0)),
        ],
        out_specs=[
            pl.BlockSpec((rows, XATTN_WIDTH), lambda i: (0, 0)),
            pl.BlockSpec((rows, XATTN_WIDTH), lambda i: (0, 0)),
        ],
        out_shape=[jax.ShapeDtypeStruct((rows, XATTN_WIDTH), BF16)] * 2,
        compiler_params=pltpu.CompilerParams(
            vmem_limit_bytes=_vmem_limit(blk, rows * 2 * XATTN_WIDTH * 8)),
        name="memkv",
    )(mem2d, g_mem, w_kv, g_xk)


def _inproj_kernel(x_ref, gn_ref, w_ref, gq_ref, gk_ref, cos_ref, sin_ref,
                   xk_ref, xv_ref, gxq_ref, gxo_ref,
                   q_out, k_out, v_out, u_out, xo_out, u_scr):
    hn = _rms(x_ref[...], gn_ref[...]).astype(BF16)
    tm = hn.shape[0]
    tn = DIFF_WIDTH

    def project(j):
        return jnp.dot(hn, w_ref[:, j * tn:(j + 1) * tn], preferred_element_type=F32)

    def qk_epilogue(acc, g_ref, out_ref, scale):
        lane = lax.broadcasted_iota(jnp.int32, (tm, LANES), 1)
        lo_comp = lane < DIFF_QK_DIM
        first_half = (lane & (DIFF_QK_DIM - 1)) < DIFF_QK_DIM // 2
        cos = cos_ref[...]
        sin = sin_ref[...]
        g = g_ref[...]
        for h in range(DIFF_HEADS):
            sl = slice(h * LANES, (h + 1) * LANES)
            t = acc[:, sl]
            t2 = t * t
            s_lo = jnp.sum(jnp.where(lo_comp, t2, 0.0), axis=-1, keepdims=True)
            s_hi = jnp.sum(jnp.where(lo_comp, 0.0, t2), axis=-1, keepdims=True)
            ms = jnp.where(lo_comp, s_lo, s_hi) * (1.0 / DIFF_QK_DIM)
            tn_ = t * lax.rsqrt(ms + EPS) * g
            fwd = pltpu.roll(tn_, DIFF_QK_DIM // 2, 1)
            bwd = pltpu.roll(tn_, LANES - DIFF_QK_DIM // 2, 1)
            rot = jnp.where(first_half, -bwd, fwd)
            out_ref[:, sl] = ((tn_ * cos + rot * sin) * scale).astype(BF16)

    qk_epilogue(project(0), gq_ref, q_out, DIFF_QK_DIM ** -0.5 * LOG2_E)
    qk_epilogue(project(1), gk_ref, k_out, 1.0)
    v_out[0] = project(2).T.astype(BF16)

    acc = project(3)
    for c in range(SSM_TILES):
        u_scr[c] = acc[:, c * LANES:(c + 1) * LANES]
        for t in range(SSM_CHUNK):
            u_out[c, :, t * LANES:(t + 1) * LANES] = (
                u_scr[c, pl.ds(t, tm // SSM_CHUNK, stride=SSM_CHUNK), :].astype(BF16))
    gxq = gxq_ref[...]
    outs = []
    ssq = jnp.zeros((tm, 1), F32)
    for hh in range(XATTN_HEADS):
        sl = slice(hh * XATTN_HEAD_DIM, (hh + 1) * XATTN_HEAD_DIM)
        t = acc[:, SSM_WIDTH + hh * XATTN_HEAD_DIM:SSM_WIDTH + (hh + 1) * XATTN_HEAD_DIM]
        qn = (_rms(t, gxq) * (XATTN_HEAD_DIM ** -0.5)).astype(BF16)
        s = lax.dot_general(qn, xk_ref[0, :, sl], _NT, preferred_element_type=F32)
        p = jnp.exp(s - jnp.max(s, axis=-1, keepdims=True))
        l = jnp.sum(p, axis=-1, keepdims=True)
        o = jnp.dot(p.astype(BF16), xv_ref[0, :, sl], preferred_element_type=F32) / l
        outs.append(o)
        ssq = ssq + jnp.sum(o * o, axis=-1, keepdims=True)
    inv = lax.rsqrt(ssq * (1.0 / XATTN_WIDTH) + EPS)
    for hh in range(XATTN_HEADS):
        sl = slice(hh * XATTN_HEAD_DIM, (hh + 1) * XATTN_HEAD_DIM)
        xo_out[:, sl] = (outs[hh] * inv * gxo_ref[:, sl]).astype(BF16)


def _inproj(x2d, seq, g_norm, w_in, g_q2, g_k2, cos2, sin2, xk, xv, g_xq, g_xo):
    rows, d = x2d.shape
    tm = ROW_TILE
    tn = DIFF_WIDTH
    n_in = w_in.shape[1]
    tiles_per_seq = seq // tm
    mem_len = xk.shape[1]
    blk = (tm * d * 4 + 2 * tm * LANES * 4 + 2 * mem_len * XATTN_WIDTH * 2
           + 3 * tm * tn * 2 + tm * SSM_WIDTH * 2 + tm * XATTN_WIDTH * 2)
    row = lambda i: (i, 0)
    const = lambda i: (0, 0)
    return pl.pallas_call(
        _inproj_kernel,
        grid=(rows // tm,),
        in_specs=[
            pl.BlockSpec((tm, d), row),
            pl.BlockSpec((1, d), const),
            pl.BlockSpec((d, n_in), const, pipeline_mode=pl.Buffered(1)),
            pl.BlockSpec((1, LANES), const),
            pl.BlockSpec((1, LANES), const),
            pl.BlockSpec((tm, LANES), lambda i: (i % tiles_per_seq, 0)),
            pl.BlockSpec((tm, LANES), lambda i: (i % tiles_per_seq, 0)),
            pl.BlockSpec((1, mem_len, XATTN_WIDTH), lambda i: (i // tiles_per_seq, 0, 0)),
            pl.BlockSpec((1, mem_len, XATTN_WIDTH), lambda i: (i // tiles_per_seq, 0, 0)),
            pl.BlockSpec((1, XATTN_HEAD_DIM), const),
            pl.BlockSpec((1, XATTN_WIDTH), const),
        ],
        out_specs=[
            pl.BlockSpec((tm, tn), row),
            pl.BlockSpec((tm, tn), row),
            pl.BlockSpec((1, tn, tm), lambda i: (i // tiles_per_seq, 0, i % tiles_per_seq)),
            pl.BlockSpec((SSM_TILES, tm // SSM_CHUNK, SSM_CHUNK * LANES), lambda i: (0, i, 0)),
            pl.BlockSpec((tm, XATTN_WIDTH), row),
        ],
        out_shape=[
            jax.ShapeDtypeStruct((rows, tn), BF16),
            jax.ShapeDtypeStruct((rows, tn), BF16),
            jax.ShapeDtypeStruct((rows // seq, tn, seq), BF16),
            jax.ShapeDtypeStruct((SSM_TILES, rows // SSM_CHUNK, SSM_CHUNK * LANES), BF16),
            jax.ShapeDtypeStruct((rows, XATTN_WIDTH), BF16),
        ],
        scratch_shapes=[pltpu.VMEM((SSM_TILES, tm, LANES), F32)],
        compiler_params=pltpu.CompilerParams(
            dimension_semantics=("parallel",),
            vmem_limit_bytes=_vmem_limit(blk, d * n_in * 2 + tm * d * 2 + 6 * tm * tn * 4)),
        name="inproj",
    )(x2d, g_norm, w_in, g_q2, g_k2, cos2, sin2, xk, xv, g_xq, g_xo)


def _attn_kernel(lq1_ref, lk1_ref, lq2_ref, lk2_ref, gsub_ref, q_ref, k_ref, vt_ref, o_ref,
                 s_scr, acc_scr, *, lambda_init):
    qi = pl.program_id(2)
    tq = q_ref.shape[1]
    tk = ATTN_KEY_TILE
    blocks_per_q = tq // tk
    q_t = q_ref[0].astype(F32).T.astype(BF16)
    feat = lax.broadcasted_iota(jnp.int32, (LANES, tq), 0)
    zero = jnp.zeros_like(q_t)
    qs = (jnp.where(feat < DIFF_QK_DIM, q_t, zero), jnp.where(feat >= DIFF_QK_DIM, q_t, zero))

    acc_scr[...] = jnp.zeros(acc_scr.shape, F32)

    def scores(kb, slot):
        k = k_ref[0, pl.ds(pl.multiple_of(kb * tk, tk), tk), :]
        for c in range(2):
            s_scr[slot, c] = jnp.dot(k, qs[c], preferred_element_type=F32)

    def softmax_pv(kb, slot, stats, diag_index=None):
        vt = vt_ref[0, :, pl.ds(pl.multiple_of(kb * tk, tk), tk)]
        vt_ones = jnp.concatenate([vt, jnp.ones((ATTN_ONES_ROWS, tk), BF16)], axis=0)
        new_stats = []
        for c in range(2):
            m_old = stats[c]
            s = s_scr[slot, c]
            if diag_index is not None:
                key = lax.broadcasted_iota(jnp.int32, (tk, tq), 0) + diag_index * tk
                qry = lax.broadcasted_iota(jnp.int32, (tk, tq), 1)
                s = jnp.where(key <= qry, s, -jnp.inf)
            m_new = jnp.maximum(m_old, jnp.max(s, axis=0, keepdims=True))
            alpha = jnp.exp2(m_old - m_new)
            p = jnp.exp2((s - m_new).astype(BF16))
            acc_scr[c] = alpha * acc_scr[c] + jnp.dot(vt_ones, p, preferred_element_type=F32)
            new_stats.append(m_new)
        return tuple(new_stats)

    def pair(i, stats):
        kb = blocks_per_q * i
        for j in range(blocks_per_q):
            scores(kb + j + 1, (j + 1) % 2)
            stats = softmax_pv(kb + j, j % 2, stats)
        return stats

    init = (jnp.full((1, tq), -jnp.inf, F32),) * 2
    def double_pair(i2, stats):
        return pair(2 * i2 + 1, pair(2 * i2, stats))

    scores(0, 0)
    stats = lax.fori_loop(0, qi // 2, double_pair, init)
    stats = lax.cond(qi % 2 == 1, lambda st: pair(qi - 1, st), lambda st: st, stats)
    kb = blocks_per_q * qi
    for j in range(blocks_per_q):
        if j + 1 < blocks_per_q:
            scores(kb + j + 1, (j + 1) % 2)
        stats = softmax_pv(kb + j, j % 2, stats, diag_index=j)

    lam = (jnp.exp(jnp.sum(lq1_ref[...] * lk1_ref[...], axis=-1, keepdims=True))
           - jnp.exp(jnp.sum(lq2_ref[...] * lk2_ref[...], axis=-1, keepdims=True))
           + lambda_init)
    nv = DIFF_V_DIM
    o_t = (acc_scr[0, 0:nv] / acc_scr[0, nv:nv + 1]
           - lam * (acc_scr[1, 0:nv] / acc_scr[1, nv:nv + 1]))
    ms = jnp.mean(o_t * o_t, axis=0, keepdims=True)
    o_t = o_t * lax.rsqrt(ms + EPS) * (gsub_ref[...] * (1.0 - lambda_init))
    o_ref[0] = o_t.T.astype(BF16)


def _attention(q, k, vt, lq1, lk1, lq2, lk2, g_subln_col, lambda_init):
    b, s, _ = q.shape
    tq = ATTN_TILE
    tk = ATTN_KEY_TILE
    assert tq % (2 * tk) == 0
    vec = pl.BlockSpec((1, DIFF_QK_DIM), lambda bi, h, qi: (0, 0))
    blk = 2 * tq * LANES * 2 + 2 * s * LANES * 2
    return pl.pallas_call(
        functools.partial(_attn_kernel, lambda_init=lambda_init),
        grid=(b, DIFF_HEADS, s // tq),
        in_specs=[
            vec, vec, vec, vec,
            pl.BlockSpec((DIFF_V_DIM, 1), lambda bi, h, qi: (0, 0)),
            pl.BlockSpec((1, tq, LANES), lambda bi, h, qi: (bi, qi, h)),
            pl.BlockSpec((1, s, LANES), lambda bi, h, qi: (bi, 0, h)),
            pl.BlockSpec((1, DIFF_V_DIM, s), lambda bi, h, qi: (bi, h, 0)),
        ],
        out_specs=pl.BlockSpec((1, tq, LANES), lambda bi, h, qi: (bi, qi, h)),
        out_shape=jax.ShapeDtypeStruct((b, s, DIFF_WIDTH), BF16),
        scratch_shapes=[
            pltpu.VMEM((2, 2, tk, tq), F32),
            pltpu.VMEM((2, DIFF_V_DIM + ATTN_ONES_ROWS, tq), F32),
        ],
        compiler_params=pltpu.CompilerParams(
            dimension_semantics=("parallel", "parallel", "arbitrary"),
            vmem_limit_bytes=_vmem_limit(blk, 2 * DIFF_V_DIM * tq * 4 + 12 * tk * tq * 4)),
        name="diffattn",
    )(lq1, lk1, lq2, lk2, g_subln_col, q, k, vt)


def _ssm_prep_kernel(lr_row, li_row, ld_row, lr_col, li_col, ld_col,
                     bre_ref, bim_ref, cre_ref, cim_ref, d_ref,
                     m_out, w_out, n_out, a_out):
    t_len = SSM_CHUNK
    ns = SSM_TILE_STATE

    def lam_bar(lr, li, ld):
        dt = jnp.exp(ld)
        e = jnp.exp(lr * dt)
        return e * jnp.cos(li * dt), e * jnp.sin(li * dt)

    lam_r = lr_row[0]
    lam_i = li_row[0]
    br, bi = lam_bar(lam_r, lam_i, ld_row[0])
    den = lam_r * lam_r + lam_i * lam_i
    nr = br - 1.0
    coef_r = (nr * lam_r + bi * lam_i) / den
    coef_i = (bi * lam_r - nr * lam_i) / den
    b_r = bre_ref[0]
    b_i = bim_ref[0]
    bb_r = coef_r * b_r - coef_i * b_i
    bb_i = coef_r * b_i + coef_i * b_r
    pr = jnp.ones_like(br)
    pi = jnp.zeros_like(br)
    for tau in range(t_len):
        rows = slice((t_len - 1 - tau) * LANES, (t_len - tau) * LANES)
        w_out[0, rows, 0:ns] = (bb_r * pr - bb_i * pi).astype(BF16)
        w_out[0, rows, ns:2 * ns] = (bb_r * pi + bb_i * pr).astype(BF16)
        pr, pi = pr * br - pi * bi, pr * bi + pi * br
    a_out[0, :, 0:ns] = pr
    a_out[0, :, ns:2 * ns] = pi

    cr, ci = lam_bar(lr_col[0], li_col[0], ld_col[0])
    c_r = cre_ref[0]
    c_i = cim_ref[0]
    qr = jnp.ones_like(cr)
    qi = jnp.zeros_like(cr)
    eye = (lax.broadcasted_iota(jnp.int32, (LANES, LANES), 0)
           == lax.broadcasted_iota(jnp.int32, (LANES, LANES), 1))
    kblk = []
    for tau in range(t_len + 1):
        cl_r = c_r * qr - c_i * qi
        cl_i = c_r * qi + c_i * qr
        if tau < t_len:
            kb = (jnp.dot(bb_r, cl_r, precision=lax.Precision.HIGHEST, preferred_element_type=F32)
                  - jnp.dot(bb_i, cl_i, precision=lax.Precision.HIGHEST, preferred_element_type=F32))
            if tau == 0:
                kb = kb + jnp.where(eye, d_ref[0], 0.0)
            kblk.append(kb.astype(BF16))
        if tau >= 1:
            cols = slice((tau - 1) * LANES, tau * LANES)
            n_out[0, 0:ns, cols] = cl_r.astype(BF16)
            n_out[0, ns:2 * ns, cols] = (-cl_i).astype(BF16)
        qr, qi = qr * cr - qi * ci, qr * ci + qi * cr
    zeros = jnp.zeros((LANES, LANES), BF16)
    for s in range(t_len):
        for t in range(t_len):
            m_out[0, s * LANES:(s + 1) * LANES, t * LANES:(t + 1) * LANES] = (
                kblk[t - s] if t >= s else zeros)


def _ssm_prep(lam_re, lam_im, log_dt, b_re, b_im, c_re, c_im, d_skip):
    nj, gj, p, h = SSM_TILES, SSM_LANE_GROUPS, SSM_STATE, SSM_GROUP_CH
    ns, tl = SSM_TILE_STATE, SSM_CHUNK * LANES
    eye = jnp.eye(gj, dtype=F32)
    dt_full = jnp.broadcast_to(log_dt[:, None], (SSM_GROUPS, p))

    def expand_b(b):
        b4 = b.reshape(nj, gj, p, h).transpose(0, 1, 3, 2)
        return (b4[:, :, :, None, :] * eye[None, :, None, :, None]).reshape(nj, LANES, ns)

    def expand_c(c):
        c4 = c.reshape(nj, gj, h, p).transpose(0, 1, 3, 2)
        return (c4[:, :, :, None, :] * eye[None, :, None, :, None]).reshape(nj, ns, LANES)

    args = (lam_re.reshape(nj, 1, ns), lam_im.reshape(nj, 1, ns), dt_full.reshape(nj, 1, ns),
            lam_re.reshape(nj, ns, 1), lam_im.reshape(nj, ns, 1), dt_full.reshape(nj, ns, 1),
            expand_b(b_re), expand_b(b_im), expand_c(c_re), expand_c(c_im),
            d_skip.reshape(nj, 1, LANES))
    spec = lambda shp: pl.BlockSpec((1,) + shp, lambda j: (j, 0, 0))
    out_bytes = tl * tl * 2 + 2 * tl * 2 * ns * 2 + 2 * ns * 4
    return pl.pallas_call(
        _ssm_prep_kernel,
        grid=(nj,),
        in_specs=[spec((1, ns))] * 3 + [spec((ns, 1))] * 3 + [spec((LANES, ns))] * 2
                 + [spec((ns, LANES))] * 2 + [spec((1, LANES))],
        out_specs=[spec((tl, tl)), spec((tl, 2 * ns)), spec((2 * ns, tl)), spec((1, 2 * ns))],
        out_shape=[
            jax.ShapeDtypeStruct((nj, tl, tl), BF16),
            jax.ShapeDtypeStruct((nj, tl, 2 * ns), BF16),
            jax.ShapeDtypeStruct((nj, 2 * ns, tl), BF16),
            jax.ShapeDtypeStruct((nj, 1, 2 * ns), F32),
        ],
        compiler_params=pltpu.CompilerParams(
            dimension_semantics=("parallel",),
            vmem_limit_bytes=_vmem_limit(out_bytes + 7 * LANES * ns * 4, 8 << 20)),
        name="ssm_prep",
    )(*args)


def _ssm_sum_kernel(u_ref, w_ref, s_out):
    s_out[0] = jnp.dot(u_ref[0], w_ref[0], preferred_element_type=F32)


def _ssm_sum(u2, w_op):
    nj, rows, tl = u2.shape
    ns2 = w_op.shape[2]
    spec = lambda shp: pl.BlockSpec((1,) + shp, lambda j: (j, 0, 0))
    blk = rows * tl * 2 + tl * ns2 * 2 + rows * ns2 * 4
    return pl.pallas_call(
        _ssm_sum_kernel,
        grid=(nj,),
        in_specs=[spec((rows, tl)), spec((tl, ns2))],
        out_specs=spec((rows, ns2)),
        out_shape=jax.ShapeDtypeStruct((nj, rows, ns2), F32),
        compiler_params=pltpu.CompilerParams(
            dimension_semantics=("parallel",), vmem_limit_bytes=_vmem_limit(blk)),
        name="ssm_sum",
    )(u2, w_op)


def _ssm_scan_kernel(s_ref, a_ref, x_out, *, n_batch, n_chunk):
    ns = SSM_TILE_STATE
    a = a_ref[0]
    ar = a[:, 0:ns]
    ai = a[:, ns:2 * ns]

    def body(c, carry):
        new = []
        for b in range(n_batch):
            xr, xi = carry[2 * b], carry[2 * b + 1]
            r = b * n_chunk + c
            x_out[0, pl.ds(r, 1), 0:ns] = xr
            x_out[0, pl.ds(r, 1), ns:2 * ns] = xi
            s = s_ref[0, pl.ds(r, 1), :]
            new.append(ar * xr - ai * xi + s[:, 0:ns])
            new.append(ar * xi + ai * xr + s[:, ns:2 * ns])
        return tuple(new)

    init = tuple(jnp.zeros((1, ns), F32) for _ in range(2 * n_batch))
    lax.fori_loop(0, n_chunk, body, init)


def _ssm_scan(s_sum, a_pow, n_batch):
    nj, rows, ns2 = s_sum.shape
    spec = lambda shp: pl.BlockSpec((1,) + shp, lambda j: (j, 0, 0))
    return pl.pallas_call(
        functools.partial(_ssm_scan_kernel, n_batch=n_batch, n_chunk=rows // n_batch),
        grid=(nj,),
        in_specs=[spec((rows, ns2)), spec((1, ns2))],
        out_specs=spec((rows, ns2)),
        out_shape=jax.ShapeDtypeStruct((nj, rows, ns2), F32),
        compiler_params=pltpu.CompilerParams(
            dimension_semantics=("parallel",),
            vmem_limit_bytes=_vmem_limit(2 * rows * ns2 * 4)),
        name="ssm_scan",
    )(s_sum, a_pow)


def _gelu_tanh(x):
    return x * (0.5 * (1.0 + jnp.tanh(math.sqrt(2.0 / math.pi) * (x + 0.044715 * (x * x * x)))))


def _ssm_out_kernel(u_ref, m_ref, x_ref, n_ref, z_out):
    xb = x_ref[0].astype(BF16)
    wide = 2 * LANES
    for n in range(SSM_CHUNK // 2):
        cols = slice(n * wide, (n + 1) * wide)
        depth = (n + 1) * wide
        y = jnp.dot(u_ref[0, :, 0:depth], m_ref[0, 0:depth, cols], preferred_element_type=F32)
        y = y + jnp.dot(xb, n_ref[0, :, cols], preferred_element_type=F32)
        z_out[0, :, cols] = _gelu_tanh(y).astype(BF16)


def _ssm_out(u2, m_op, x_state, n_op):
    nj, rows, tl = u2.shape
    ns2 = x_state.shape[2]
    n_split = 2
    tr = rows // n_split
    rspec = lambda shp: pl.BlockSpec((1,) + shp, lambda j, r: (j, r, 0))
    cspec = lambda shp: pl.BlockSpec((1,) + shp, lambda j, r: (j, 0, 0))
    blk = tr * tl * 2 + tl * tl * 2 + tr * ns2 * 4 + ns2 * tl * 2 + tr * tl * 2
    return pl.pallas_call(
        _ssm_out_kernel,
        grid=(nj, n_split),
        in_specs=[rspec((tr, tl)), cspec((tl, tl)), rspec((tr, ns2)), cspec((ns2, tl))],
        out_specs=rspec((tr, tl)),
        out_shape=jax.ShapeDtypeStruct((nj, rows, tl), BF16),
        compiler_params=pltpu.CompilerParams(
            dimension_semantics=("parallel", "arbitrary"),
            vmem_limit_bytes=_vmem_limit(blk, 3 * tr * tl * 4)),
        name="ssm_out",
    )(u2, m_op, x_state, n_op)


def _outproj_kernel(a_ref, z_ref, xo_ref, x_ref, wglu_ref, bglu_ref, gs_ref, wo_ref, gm_ref,
                    h_out, hm_out, z_scr):
    tm = x_ref.shape[0]
    half = tm // 2
    halves = [slice(r * half, (r + 1) * half) for r in range(2)]
    for c in range(SSM_TILES):
        for t in range(SSM_CHUNK):
            z_scr[c, pl.ds(t, tm // SSM_CHUNK, stride=SSM_CHUNK), :] = (
                z_ref[c, :, t * LANES:(t + 1) * LANES].astype(F32))
    zs = [jnp.concatenate([z_scr[c, rows, :] for c in range(SSM_TILES)], axis=-1).astype(BF16)
          for rows in halves]
    gates = [jnp.dot(z, wglu_ref[...], preferred_element_type=F32) + bglu_ref[...] for z in zs]
    for rows, z, gate_in in zip(halves, zs, gates):
        acc = jnp.dot(a_ref[rows, :], wo_ref[0:DIFF_WIDTH, :], preferred_element_type=F32)
        acc = acc + jnp.dot(xo_ref[rows, :], wo_ref[DIFF_WIDTH + SSM_WIDTH:, :],
                            preferred_element_type=F32)
        zg = z.astype(F32) * (1.0 / (1.0 + jnp.exp(-gate_in)))
        s_out = _rms(zg, gs_ref[...]).astype(BF16)
        acc = acc + jnp.dot(s_out, wo_ref[DIFF_WIDTH:DIFF_WIDTH + SSM_WIDTH, :],
                            preferred_element_type=F32)
        h = x_ref[rows, :] + acc
        h_out[rows, :] = h
        hm_out[rows, :] = _rms(h, gm_ref[...]).astype(BF16)


def _outproj(a_out, z, xo, x2d, w_glu, b_glu, g_ssm, w_out, g_mlp):
    rows, d = x2d.shape
    tm = ROW_TILE
    mix = w_out.shape[0]
    row = lambda i: (i, 0)
    const = lambda i: (0, 0)
    blk = (tm * DIFF_WIDTH * 2 + tm * SSM_WIDTH * 2 + tm * XATTN_WIDTH * 2 + tm * d * 4
           + SSM_WIDTH * SSM_WIDTH * 2 + mix * d * 2 + tm * d * 4 + tm * d * 2)
    return pl.pallas_call(
        _outproj_kernel,
        grid=(rows // tm,),
        in_specs=[
            pl.BlockSpec((tm, DIFF_WIDTH), row),
            pl.BlockSpec((SSM_TILES, tm // SSM_CHUNK, SSM_CHUNK * LANES), lambda i: (0, i, 0)),
            pl.BlockSpec((tm, XATTN_WIDTH), row),
            pl.BlockSpec((tm, d), row),
            pl.BlockSpec((SSM_WIDTH, SSM_WIDTH), const),
            pl.BlockSpec((1, SSM_WIDTH), const),
            pl.BlockSpec((1, SSM_WIDTH), const),
            pl.BlockSpec((mix, d), const),
            pl.BlockSpec((1, d), const),
        ],
        out_specs=[pl.BlockSpec((tm, d), row), pl.BlockSpec((tm, d), row)],
        out_shape=[jax.ShapeDtypeStruct((rows, d), F32), jax.ShapeDtypeStruct((rows, d), BF16)],
        scratch_shapes=[pltpu.VMEM((SSM_TILES, tm, LANES), F32)],
        compiler_params=pltpu.CompilerParams(
            dimension_semantics=("parallel",),
            vmem_limit_bytes=_vmem_limit(blk, 3 * tm * d * 4)),
        name="outproj",
    )(a_out, z, xo, x2d, w_glu, b_glu, g_ssm, w_out, g_mlp)


def _mlp_kernel(hm_ref, w1_ref, w2_ref, h_ref, o_ref):
    f = pl.program_id(1)

    @pl.when(f == 0)
    def _():
        o_ref[...] = h_ref[...]

    ff = jnp.dot(hm_ref[...], w1_ref[...], preferred_element_type=F32)
    ff = jnp.square(jnp.maximum(ff, 0.0)).astype(BF16)
    o_ref[...] += jnp.dot(ff, w2_ref[...], preferred_element_type=F32)


def _mlp(hm, h, w1, w2):
    rows, d = h.shape
    d_ff = w1.shape[1]
    tm, tf = ROW_TILE, MLP_FF_TILE
    blk = tm * d * 2 + d * tf * 2 + tf * d * 2 + tm * d * 4 + tm * d * 4
    return pl.pallas_call(
        _mlp_kernel,
        grid=(rows // tm, d_ff // tf),
        in_specs=[
            pl.BlockSpec((tm, d), lambda i, f: (i, 0)),
            pl.BlockSpec((d, tf), lambda i, f: (0, f)),
            pl.BlockSpec((tf, d), lambda i, f: (f, 0)),
            pl.BlockSpec((tm, d), lambda i, f: (i, 0)),
        ],
        out_specs=pl.BlockSpec((tm, d), lambda i, f: (i, 0)),
        out_shape=jax.ShapeDtypeStruct((rows, d), F32),
        compiler_params=pltpu.CompilerParams(
            dimension_semantics=("parallel", "arbitrary"),
            vmem_limit_bytes=_vmem_limit(blk, tm * tf * 6 + tm * d * 4)),
        name="mlp",
    )(hm, w1, w2, h)


def _rotary_tables(seq_len):
    dim = DIFF_QK_DIM
    inv_freq = 1.0 / (ROPE_THETA ** (jnp.arange(0, dim, 2, dtype=F32) / dim))
    freqs = jnp.arange(seq_len, dtype=F32)[:, None] * inv_freq[None, :]
    emb = jnp.concatenate([freqs, freqs, freqs, freqs], axis=-1)
    return jnp.cos(emb), jnp.sin(emb)


def kernel(x, mem, g_attn_norm, w_in, g_q, g_k, lam_q1, lam_k1, lam_q2, lam_k2, g_subln, lam_re, lam_im, log_dt, b_re, b_im, c_re, c_im, d_skip, w_glu, b_glu, g_ssm_out, g_mem, w_mem_kv, g_xq, g_xk, g_xattn_out, w_out, g_mlp_norm, w_mlp_in, w_mlp_out):
    b, s, d = x.shape
    m = mem.shape[1]
    depth = w_in.shape[0]
    rows = b * s
    cos2, sin2 = _rotary_tables(s)
    h = x.reshape(rows, d)
    mem2d = mem.reshape(b * m, d)
    row2 = lambda v: v.reshape(1, -1)
    for layer in range(depth):
        lambda_init = 0.8 - 0.6 * math.exp(-0.3 * layer)

        xk, xv = _memkv(mem2d, row2(g_mem[layer]), w_mem_kv[layer].astype(BF16),
                        row2(g_xk[layer]))
        q, k, vt, u, x_out = _inproj(
            h, s, row2(g_attn_norm[layer]), w_in[layer].astype(BF16),
            row2(jnp.tile(g_q[layer], 2)), row2(jnp.tile(g_k[layer], 2)), cos2, sin2,
            xk.reshape(b, m, XATTN_WIDTH), xv.reshape(b, m, XATTN_WIDTH),
            row2(g_xq[layer]), row2(g_xattn_out[layer]))

        a_out = _attention(q.reshape(b, s, DIFF_WIDTH), k.reshape(b, s, DIFF_WIDTH),
                           vt, row2(lam_q1[layer]), row2(lam_k1[layer]),
                           row2(lam_q2[layer]), row2(lam_k2[layer]),
                           g_subln[layer].reshape(DIFF_V_DIM, 1), lambda_init)

        m_op, w_op, n_op, a_pow = _ssm_prep(lam_re[layer], lam_im[layer], log_dt[layer],
                                            b_re[layer], b_im[layer], c_re[layer], c_im[layer],
                                            d_skip[layer])
        x_state = _ssm_scan(_ssm_sum(u, w_op), a_pow, b)
        z = _ssm_out(u, m_op, x_state, n_op)

        h, hm = _outproj(a_out.reshape(rows, DIFF_WIDTH), z, x_out, h,
                         w_glu[layer].astype(BF16), row2(b_glu[layer]), row2(g_ssm_out[layer]),
                         w_out[layer].astype(BF16), row2(g_mlp_norm[layer]))
        h = _mlp(hm, h, w_mlp_in[layer].astype(BF16), w_mlp_out[layer].astype(BF16))
    return h.reshape(b, s, d)
```

```python
import functools
import math

import jax
import jax.numpy as jnp
from jax import lax
from jax.experimental import pallas as pl
from jax.experimental.pallas import tpu as pltpu

F32 = jnp.float32
BF16 = jnp.bfloat16

LANES = 128
VMEM_CAP_BYTES = 60000 * 1024

DIFF_HEADS = 8
DIFF_QK_DIM = 64
DIFF_V_DIM = 2 * DIFF_QK_DIM
DIFF_WIDTH = DIFF_HEADS * DIFF_V_DIM
SSM_GROUPS = 32
SSM_GROUP_CH = 16
SSM_STATE = 64
SSM_WIDTH = SSM_GROUPS * SSM_GROUP_CH
XATTN_HEADS = 4
XATTN_HEAD_DIM = 128
XATTN_WIDTH = XATTN_HEADS * XATTN_HEAD_DIM
ROPE_THETA = 10000.0
EPS = 1e-6
LOG2_E = math.log2(math.e)

SSM_CHUNK = 16
SSM_LANE_GROUPS = LANES // SSM_GROUP_CH
SSM_TILES = SSM_WIDTH // LANES
SSM_TILE_STATE = SSM_LANE_GROUPS * SSM_STATE

ROW_TILE = 512
ATTN_TILE = 512
ATTN_KEY_TILE = 256
ATTN_ONES_ROWS = 16
MLP_FF_TILE = 2048

_NT = (((1,), (1,)), ((), ()))


def _vmem_limit(block_bytes, scratch_bytes=0):
    return int(min(2 * block_bytes + scratch_bytes + (4 << 20), VMEM_CAP_BYTES))


def _rms(t, g):
    ms = jnp.mean(t * t, axis=-1, keepdims=True)
    return t * lax.rsqrt(ms + EPS) * g


def _memkv_kernel(mem_ref, g_ref, w_ref, gxk_ref, k_out, v_out):
    hn = _rms(mem_ref[...], g_ref[...]).astype(BF16)
    kv = jnp.dot(hn, w_ref[...], preferred_element_type=F32)
    gxk = gxk_ref[...]
    for hh in range(XATTN_HEADS):
        sl = slice(hh * XATTN_HEAD_DIM, (hh + 1) * XATTN_HEAD_DIM)
        k_out[:, sl] = _rms(kv[:, sl], gxk).astype(BF16)
    v_out[...] = kv[:, XATTN_WIDTH:].astype(BF16)


def _memkv(mem2d, g_mem, w_kv, g_xk):
    rows, d = mem2d.shape
    blk = rows * d * 4 + d * 2 * XATTN_WIDTH * 2 + 2 * rows * XATTN_WIDTH * 2
    return pl.pallas_call(
        _memkv_kernel,
        grid=(1,),
        in_specs=[
            pl.BlockSpec((rows, d), lambda i: (0, 0)),
            pl.BlockSpec((1, d), lambda i: (0, 0)),
            pl.BlockSpec((d, 2 * XATTN_WIDTH), lambda i: (0, 0)),
            pl.BlockSpec((1, XATTN_HEAD_DIM), lambda i: (0, 0)),
        ],
        out_specs=[
            pl.BlockSpec((rows, XATTN_WIDTH), lambda i: (0, 0)),
            pl.BlockSpec((rows, XATTN_WIDTH), lambda i: (0, 0)),
        ],
        out_shape=[jax.ShapeDtypeStruct((rows, XATTN_WIDTH), BF16)] * 2,
        compiler_params=pltpu.CompilerParams(
            vmem_limit_bytes=_vmem_limit(blk, rows * 2 * XATTN_WIDTH * 8)),
        name="memkv",
    )(mem2d, g_mem, w_kv, g_xk)


def _inproj_kernel(x_ref, gn_ref, w_ref, gq_ref, gk_ref, cos_ref, sin_ref,
                   xk_ref, xv_ref, gxq_ref, gxo_ref,
                   q_out, k_out, v_out, u_out, xo_out, u_scr):
    hn = _rms(x_ref[...], gn_ref[...]).astype(BF16)
    tm = hn.shape[0]
    tn = DIFF_WIDTH

    def project(j):
        return jnp.dot(hn, w_ref[:, j * tn:(j + 1) * tn], preferred_element_type=F32)

    def qk_epilogue(acc, g_ref, out_ref, scale):
        lane = lax.broadcasted_iota(jnp.int32, (tm, LANES), 1)
        lo_comp = lane < DIFF_QK_DIM
        first_half = (lane & (DIFF_QK_DIM - 1)) < DIFF_QK_DIM // 2
        cos = cos_ref[...]
        sin = sin_ref[...]
        g = g_ref[...]
        for h in range(DIFF_HEADS):
            sl = slice(h * LANES, (h + 1) * LANES)
            t = acc[:, sl]
            t2 = t * t
            s_lo = jnp.sum(jnp.where(lo_comp, t2, 0.0), axis=-1, keepdims=True)
            s_hi = jnp.sum(jnp.where(lo_comp, 0.0, t2), axis=-1, keepdims=True)
            ms = jnp.where(lo_comp, s_lo, s_hi) * (1.0 / DIFF_QK_DIM)
            tn_ = t * lax.rsqrt(ms + EPS) * g
            fwd = pltpu.roll(tn_, DIFF_QK_DIM // 2, 1)
            bwd = pltpu.roll(tn_, LANES - DIFF_QK_DIM // 2, 1)
            rot = jnp.where(first_half, -bwd, fwd)
            out_ref[:, sl] = ((tn_ * cos + rot * sin) * scale).astype(BF16)

    qk_epilogue(project(0), gq_ref, q_out, DIFF_QK_DIM ** -0.5 * LOG2_E)
    qk_epilogue(project(1), gk_ref, k_out, 1.0)
    v_out[0] = project(2).T.astype(BF16)

    acc = project(3)
    for c in range(SSM_TILES):
        u_scr[c] = acc[:, c * LANES:(c + 1) * LANES]
        for t in range(SSM_CHUNK):
            u_out[c, :, t * LANES:(t + 1) * LANES] = (
                u_scr[c, pl.ds(t, tm // SSM_CHUNK, stride=SSM_CHUNK), :].astype(BF16))
    gxq = gxq_ref[...]
    outs = []
    ssq = jnp.zeros((tm, 1), F32)
    for hh in range(XATTN_HEADS):
        sl = slice(hh * XATTN_HEAD_DIM, (hh + 1) * XATTN_HEAD_DIM)
        t = acc[:, SSM_WIDTH + hh * XATTN_HEAD_DIM:SSM_WIDTH + (hh + 1) * XATTN_HEAD_DIM]
        qn = (_rms(t, gxq) * (XATTN_HEAD_DIM ** -0.5)).astype(BF16)
        s = lax.dot_general(qn, xk_ref[0, :, sl], _NT, preferred_element_type=F32)
        p = jnp.exp(s - jnp.max(s, axis=-1, keepdims=True))
        l = jnp.sum(p, axis=-1, keepdims=True)
        o = jnp.dot(p.astype(BF16), xv_ref[0, :, sl], preferred_element_type=F32) / l
        outs.append(o)
        ssq = ssq + jnp.sum(o * o, axis=-1, keepdims=True)
    inv = lax.rsqrt(ssq * (1.0 / XATTN_WIDTH) + EPS)
    for hh in range(XATTN_HEADS):
        sl = slice(hh * XATTN_HEAD_DIM, (hh + 1) * XATTN_HEAD_DIM)
        xo_out[:, sl] = (outs[hh] * inv * gxo_ref[:, sl]).astype(BF16)


def _inproj(x2d, seq, g_norm, w_in, g_q2, g_k2, cos2, sin2, xk, xv, g_xq, g_xo):
    rows, d = x2d.shape
    tm = ROW_TILE
    tn = DIFF_WIDTH
    n_in = w_in.shape[1]
    tiles_per_seq = seq // tm
    mem_len = xk.shape[1]
    blk = (tm * d * 4 + 2 * tm * LANES * 4 + 2 * mem_len * XATTN_WIDTH * 2
           + 3 * tm * tn * 2 + tm * SSM_WIDTH * 2 + tm * XATTN_WIDTH * 2)
    row = lambda i: (i, 0)
    const = lambda i: (0, 0)
    return pl.pallas_call(
        _inproj_kernel,
        grid=(rows // tm,),
        in_specs=[
            pl.BlockSpec((tm, d), row),
            pl.BlockSpec((1, d), const),
            pl.BlockSpec((d, n_in), const, pipeline_mode=pl.Buffered(1)),
            pl.BlockSpec((1, LANES), const),
            pl.BlockSpec((1, LANES), const),
            pl.BlockSpec((tm, LANES), lambda i: (i % tiles_per_seq, 0)),
            pl.BlockSpec((tm, LANES), lambda i: (i % tiles_per_seq, 0)),
            pl.BlockSpec((1, mem_len, XATTN_WIDTH), lambda i: (i // tiles_per_seq, 0, 0)),
            pl.BlockSpec((1, mem_len, XATTN_WIDTH), lambda i: (i // tiles_per_seq, 0, 0)),
            pl.BlockSpec((1, XATTN_HEAD_DIM), const),
            pl.BlockSpec((1, XATTN_WIDTH), const),
        ],
        out_specs=[
            pl.BlockSpec((tm, tn), row),
            pl.BlockSpec((tm, tn), row),
            pl.BlockSpec((1, tn, tm), lambda i: (i // tiles_per_seq, 0, i % tiles_per_seq)),
            pl.BlockSpec((SSM_TILES, tm // SSM_CHUNK, SSM_CHUNK * LANES), lambda i: (0, i, 0)),
            pl.BlockSpec((tm, XATTN_WIDTH), row),
        ],
        out_shape=[
            jax.ShapeDtypeStruct((rows, tn), BF16),
            jax.ShapeDtypeStruct((rows, tn), BF16),
            jax.ShapeDtypeStruct((rows // seq, tn, seq), BF16),
            jax.ShapeDtypeStruct((SSM_TILES, rows // SSM_CHUNK, SSM_CHUNK * LANES), BF16),
            jax.ShapeDtypeStruct((rows, XATTN_WIDTH), BF16),
        ],
        scratch_shapes=[pltpu.VMEM((SSM_TILES, tm, LANES), F32)],
        compiler_params=pltpu.CompilerParams(
            dimension_semantics=("parallel",),
            vmem_limit_bytes=_vmem_limit(blk, d * n_in * 2 + tm * d * 2 + 6 * tm * tn * 4)),
        name="inproj",
    )(x2d, g_norm, w_in, g_q2, g_k2, cos2, sin2, xk, xv, g_xq, g_xo)


def _attn_kernel(lq1_ref, lk1_ref, lq2_ref, lk2_ref, gsub_ref, q_ref, k_ref, vt_ref, o_ref,
                 s_scr, acc_scr, *, lambda_init):
    qi = pl.program_id(2)
    tq = q_ref.shape[1]
    tk = ATTN_KEY_TILE
    blocks_per_q = tq // tk
    q_t = q_ref[0].astype(F32).T.astype(BF16)
    feat = lax.broadcasted_iota(jnp.int32, (LANES, tq), 0)
    zero = jnp.zeros_like(q_t)
    qs = (jnp.where(feat < DIFF_QK_DIM, q_t, zero), jnp.where(feat >= DIFF_QK_DIM, q_t, zero))

    acc_scr[...] = jnp.zeros(acc_scr.shape, F32)

    def scores(kb, slot):
        k = k_ref[0, pl.ds(pl.multiple_of(kb * tk, tk), tk), :]
        for c in range(2):
            s_scr[slot, c] = jnp.dot(k, qs[c], preferred_element_type=F32)

    def softmax_pv(kb, slot, stats, diag_index=None):
        vt = vt_ref[0, :, pl.ds(pl.multiple_of(kb * tk, tk), tk)]
        vt_ones = jnp.concatenate([vt, jnp.ones((ATTN_ONES_ROWS, tk), BF16)], axis=0)
        new_stats = []
        for c in range(2):
            m_old = stats[c]
            s = s_scr[slot, c]
            if diag_index is not None:
                key = lax.broadcasted_iota(jnp.int32, (tk, tq), 0) + diag_index * tk
                qry = lax.broadcasted_iota(jnp.int32, (tk, tq), 1)
                s = jnp.where(key <= qry, s, -jnp.inf)
            m_new = jnp.maximum(m_old, jnp.max(s, axis=0, keepdims=True))
            alpha = jnp.exp2(m_old - m_new)
            p = jnp.exp2((s - m_new).astype(BF16))
            acc_scr[c] = alpha * acc_scr[c] + jnp.dot(vt_ones, p, preferred_element_type=F32)
            new_stats.append(m_new)
        return tuple(new_stats)

    def pair(i, stats):
        kb = blocks_per_q * i
        for j in range(blocks_per_q):
            scores(kb + j + 1, (j + 1) % 2)
            stats = softmax_pv(kb + j, j % 2, stats)
        return stats

    init = (jnp.full((1, tq), -jnp.inf, F32),) * 2
    def double_pair(i2, stats):
        return pair(2 * i2 + 1, pair(2 * i2, stats))

    scores(0, 0)
    stats = lax.fori_loop(0, qi // 2, double_pair, init)
    stats = lax.cond(qi % 2 == 1, lambda st: pair(qi - 1, st), lambda st: st, stats)
    kb = blocks_per_q * qi
    for j in range(blocks_per_q):
        if j + 1 < blocks_per_q:
            scores(kb + j + 1, (j + 1) % 2)
        stats = softmax_pv(kb + j, j % 2, stats, diag_index=j)

    lam = (jnp.exp(jnp.sum(lq1_ref[...] * lk1_ref[...], axis=-1, keepdims=True))
           - jnp.exp(jnp.sum(lq2_ref[...] * lk2_ref[...], axis=-1, keepdims=True))
           + lambda_init)
    nv = DIFF_V_DIM
    o_t = (acc_scr[0, 0:nv] / acc_scr[0, nv:nv + 1]
           - lam * (acc_scr[1, 0:nv] / acc_scr[1, nv:nv + 1]))
    ms = jnp.mean(o_t * o_t, axis=0, keepdims=True)
    o_t = o_t * lax.rsqrt(ms + EPS) * (gsub_ref[...] * (1.0 - lambda_init))
    o_ref[0] = o_t.T.astype(BF16)


def _attention(q, k, vt, lq1, lk1, lq2, lk2, g_subln_col, lambda_init):
    b, s, _ = q.shape
    tq = ATTN_TILE
    tk = ATTN_KEY_TILE
    assert tq % (2 * tk) == 0
    vec = pl.BlockSpec((1, DIFF_QK_DIM), lambda bi, h, qi: (0, 0))
    blk = 2 * tq * LANES * 2 + 2 * s * LANES * 2
    return pl.pallas_call(
        functools.partial(_attn_kernel, lambda_init=lambda_init),
        grid=(b, DIFF_HEADS, s // tq),
        in_specs=[
            vec, vec, vec, vec,
            pl.BlockSpec((DIFF_V_DIM, 1), lambda bi, h, qi: (0, 0)),
            pl.BlockSpec((1, tq, LANES), lambda bi, h, qi: (bi, qi, h)),
            pl.BlockSpec((1, s, LANES), lambda bi, h, qi: (bi, 0, h)),
            pl.BlockSpec((1, DIFF_V_DIM, s), lambda bi, h, qi: (bi, h, 0)),
        ],
        out_specs=pl.BlockSpec((1, tq, LANES), lambda bi, h, qi: (bi, qi, h)),
        out_shape=jax.ShapeDtypeStruct((b, s, DIFF_WIDTH), BF16),
        scratch_shapes=[
            pltpu.VMEM((2, 2, tk, tq), F32),
            pltpu.VMEM((2, DIFF_V_DIM + ATTN_ONES_ROWS, tq), F32),
        ],
        compiler_params=pltpu.CompilerParams(
            dimension_semantics=("parallel", "parallel", "arbitrary"),
            vmem_limit_bytes=_vmem_limit(blk, 2 * DIFF_V_DIM * tq * 4 + 12 * tk * tq * 4)),
        name="diffattn",
    )(lq1, lk1, lq2, lk2, g_subln_col, q, k, vt)


def _ssm_prep_kernel(lr_row, li_row, ld_row, lr_col, li_col, ld_col,
                     bre_ref, bim_ref, cre_ref, cim_ref, d_ref,
                     m_out, w_out, n_out, a_out):
    t_len = SSM_CHUNK
    ns = SSM_TILE_STATE

    def lam_bar(lr, li, ld):
        dt = jnp.exp(ld)
        e = jnp.exp(lr * dt)
        return e * jnp.cos(li * dt), e * jnp.sin(li * dt)

    lam_r = lr_row[0]
    lam_i = li_row[0]
    br, bi = lam_bar(lam_r, lam_i, ld_row[0])
    den = lam_r * lam_r + lam_i * lam_i
    nr = br - 1.0
    coef_r = (nr * lam_r + bi * lam_i) / den
    coef_i = (bi * lam_r - nr * lam_i) / den
    b_r = bre_ref[0]
    b_i = bim_ref[0]
    bb_r = coef_r * b_r - coef_i * b_i
    bb_i = coef_r * b_i + coef_i * b_r
    pr = jnp.ones_like(br)
    pi = jnp.zeros_like(br)
    for tau in range(t_len):
        rows = slice((t_len - 1 - tau) * LANES, (t_len - tau) * LANES)
        w_out[0, rows, 0:ns] = (bb_r * pr - bb_i * pi).astype(BF16)
        w_out[0, rows, ns:2 * ns] = (bb_r * pi + bb_i * pr).astype(BF16)
        pr, pi = pr * br - pi * bi, pr * bi + pi * br
    a_out[0, :, 0:ns] = pr
    a_out[0, :, ns:2 * ns] = pi

    cr, ci = lam_bar(lr_col[0], li_col[0], ld_col[0])
    c_r = cre_ref[0]
    c_i = cim_ref[0]
    qr = jnp.ones_like(cr)
    qi = jnp.zeros_like(cr)
    eye = (lax.broadcasted_iota(jnp.int32, (LANES, LANES), 0)
           == lax.broadcasted_iota(jnp.int32, (LANES, LANES), 1))
    kblk = []
    for tau in range(t_len + 1):
        cl_r = c_r * qr - c_i * qi
        cl_i = c_r * qi + c_i * qr
        if tau < t_len:
            kb = (jnp.dot(bb_r, cl_r, precision=lax.Precision.HIGHEST, preferred_element_type=F32)
                  - jnp.dot(bb_i, cl_i, precision=lax.Precision.HIGHEST, preferred_element_type=F32))
            if tau == 0:
                kb = kb + jnp.where(eye, d_ref[0], 0.0)
            kblk.append(kb.astype(BF16))
        if tau >= 1:
            cols = slice((tau - 1) * LANES, tau * LANES)
            n_out[0, 0:ns, cols] = cl_r.astype(BF16)
            n_out[0, ns:2 * ns, cols] = (-cl_i).astype(BF16)
        qr, qi = qr * cr - qi * ci, qr * ci + qi * cr
    zeros = jnp.zeros((LANES, LANES), BF16)
    for s in range(t_len):
        for t in range(t_len):
            m_out[0, s * LANES:(s + 1) * LANES, t * LANES:(t + 1) * LANES] = (
                kblk[t - s] if t >= s else zeros)


def _ssm_prep(lam_re, lam_im, log_dt, b_re, b_im, c_re, c_im, d_skip):
    nj, gj, p, h = SSM_TILES, SSM_LANE_GROUPS, SSM_STATE, SSM_GROUP_CH
    ns, tl = SSM_TILE_STATE, SSM_CHUNK * LANES
    eye = jnp.eye(gj, dtype=F32)
    dt_full = jnp.broadcast_to(log_dt[:, None], (SSM_GROUPS, p))

    def expand_b(b):
        b4 = b.reshape(nj, gj, p, h).transpose(0, 1, 3, 2)
        return (b4[:, :, :, None, :] * eye[None, :, None, :, None]).reshape(nj, LANES, ns)

    def expand_c(c):
        c4 = c.reshape(nj, gj, h, p).transpose(0, 1, 3, 2)
        return (c4[:, :, :, None, :] * eye[None, :, None, :, None]).reshape(nj, ns, LANES)

    args = (lam_re.reshape(nj, 1, ns), lam_im.reshape(nj, 1, ns), dt_full.reshape(nj, 1, ns),
            lam_re.reshape(nj, ns, 1), lam_im.reshape(nj, ns, 1), dt_full.reshape(nj, ns, 1),
            expand_b(b_re), expand_b(b_im), expand_c(c_re), expand_c(c_im),
            d_skip.reshape(nj, 1, LANES))
    spec = lambda shp: pl.BlockSpec((1,) + shp, lambda j: (j, 0, 0))
    out_bytes = tl * tl * 2 + 2 * tl * 2 * ns * 2 + 2 * ns * 4
    return pl.pallas_call(
        _ssm_prep_kernel,
        grid=(nj,),
        in_specs=[spec((1, ns))] * 3 + [spec((ns, 1))] * 3 + [spec((LANES, ns))] * 2
                 + [spec((ns, LANES))] * 2 + [spec((1, LANES))],
        out_specs=[spec((tl, tl)), spec((tl, 2 * ns)), spec((2 * ns, tl)), spec((1, 2 * ns))],
        out_shape=[
            jax.ShapeDtypeStruct((nj, tl, tl), BF16),
            jax.ShapeDtypeStruct((nj, tl, 2 * ns), BF16),
            jax.ShapeDtypeStruct((nj, 2 * ns, tl), BF16),
            jax.ShapeDtypeStruct((nj, 1, 2 * ns), F32),
        ],
        compiler_params=pltpu.CompilerParams(
            dimension_semantics=("parallel",),
            vmem_limit_bytes=_vmem_limit(out_bytes + 7 * LANES * ns * 4, 8 << 20)),
        name="ssm_prep",
    )(*args)


def _ssm_sum_kernel(u_ref, w_ref, s_out):
    s_out[0] = jnp.dot(u_ref[0], w_ref[0], preferred_element_type=F32)


def _ssm_sum(u2, w_op):
    nj, rows, tl = u2.shape
    ns2 = w_op.shape[2]
    spec = lambda shp: pl.BlockSpec((1,) + shp, lambda j: (j, 0, 0))
    blk = rows * tl * 2 + tl * ns2 * 2 + rows * ns2 * 4
    return pl.pallas_call(
        _ssm_sum_kernel,
        grid=(nj,),
        in_specs=[spec((rows, tl)), spec((tl, ns2))],
        out_specs=spec((rows, ns2)),
        out_shape=jax.ShapeDtypeStruct((nj, rows, ns2), F32),
        compiler_params=pltpu.CompilerParams(
            dimension_semantics=("parallel",), vmem_limit_bytes=_vmem_limit(blk)),
        name="ssm_sum",
    )(u2, w_op)


def _ssm_scan_kernel(s_ref, a_ref, x_out, *, n_batch, n_chunk):
    ns = SSM_TILE_STATE
    a = a_ref[0]
    ar = a[:, 0:ns]
    ai = a[:, ns:2 * ns]

    def body(c, carry):
        new = []
        for b in range(n_batch):
            xr, xi = carry[2 * b], carry[2 * b + 1]
            r = b * n_chunk + c
            x_out[0, pl.ds(r, 1), 0:ns] = xr
            x_out[0, pl.ds(r, 1), ns:2 * ns] = xi
            s = s_ref[0, pl.ds(r, 1), :]
            new.append(ar * xr - ai * xi + s[:, 0:ns])
            new.append(ar * xi + ai * xr + s[:, ns:2 * ns])
        return tuple(new)

    init = tuple(jnp.zeros((1, ns), F32) for _ in range(2 * n_batch))
    lax.fori_loop(0, n_chunk, body, init)


def _ssm_scan(s_sum, a_pow, n_batch):
    nj, rows, ns2 = s_sum.shape
    spec = lambda shp: pl.BlockSpec((1,) + shp, lambda j: (j, 0, 0))
    return pl.pallas_call(
        functools.partial(_ssm_scan_kernel, n_batch=n_batch, n_chunk=rows // n_batch),
        grid=(nj,),
        in_specs=[spec((rows, ns2)), spec((1, ns2))],
        out_specs=spec((rows, ns2)),
        out_shape=jax.ShapeDtypeStruct((nj, rows, ns2), F32),
        compiler_params=pltpu.CompilerParams(
            dimension_semantics=("parallel",),
            vmem_limit_bytes=_vmem_limit(2 * rows * ns2 * 4)),
        name="ssm_scan",
    )(s_sum, a_pow)


def _gelu_tanh(x):
    return x * (0.5 * (1.0 + jnp.tanh(math.sqrt(2.0 / math.pi) * (x + 0.044715 * (x * x * x)))))


def _ssm_out_kernel(u_ref, m_ref, x_ref, n_ref, z_out):
    xb = x_ref[0].astype(BF16)
    wide = 2 * LANES
    for n in range(SSM_CHUNK // 2):
        cols = slice(n * wide, (n + 1) * wide)
        depth = (n + 1) * wide
        y = jnp.dot(u_ref[0, :, 0:depth], m_ref[0, 0:depth, cols], preferred_element_type=F32)
        y = y + jnp.dot(xb, n_ref[0, :, cols], preferred_element_type=F32)
        z_out[0, :, cols] = _gelu_tanh(y).astype(BF16)


def _ssm_out(u2, m_op, x_state, n_op):
    nj, rows, tl = u2.shape
    ns2 = x_state.shape[2]
    n_split = 2
    tr = rows // n_split
    rspec = lambda shp: pl.BlockSpec((1,) + shp, lambda j, r: (j, r, 0))
    cspec = lambda shp: pl.BlockSpec((1,) + shp, lambda j, r: (j, 0, 0))
    blk = tr * tl * 2 + tl * tl * 2 + tr * ns2 * 4 + ns2 * tl * 2 + tr * tl * 2
    return pl.pallas_call(
        _ssm_out_kernel,
        grid=(nj, n_split),
        in_specs=[rspec((tr, tl)), cspec((tl, tl)), rspec((tr, ns2)), cspec((ns2, tl))],
        out_specs=rspec((tr, tl)),
        out_shape=jax.ShapeDtypeStruct((nj, rows, tl), BF16),
        compiler_params=pltpu.CompilerParams(
            dimension_semantics=("parallel", "arbitrary"),
            vmem_limit_bytes=_vmem_limit(blk, 3 * tr * tl * 4)),
        name="ssm_out",
    )(u2, m_op, x_state, n_op)


def _outproj_kernel(a_ref, z_ref, xo_ref, x_ref, wglu_ref, bglu_ref, gs_ref, wo_ref, gm_ref,
                    h_out, hm_out, z_scr):
    tm = x_ref.shape[0]
    half = tm // 2
    halves = [slice(r * half, (r + 1) * half) for r in range(2)]
    for c in range(SSM_TILES):
        for t in range(SSM_CHUNK):
            z_scr[c, pl.ds(t, tm // SSM_CHUNK, stride=SSM_CHUNK), :] = (
                z_ref[c, :, t * LANES:(t + 1) * LANES].astype(F32))
    zs = [jnp.concatenate([z_scr[c, rows, :] for c in range(SSM_TILES)], axis=-1).astype(BF16)
          for rows in halves]
    gates = [jnp.dot(z, wglu_ref[...], preferred_element_type=F32) + bglu_ref[...] for z in zs]
    for rows, z, gate_in in zip(halves, zs, gates):
        acc = jnp.dot(a_ref[rows, :], wo_ref[0:DIFF_WIDTH, :], preferred_element_type=F32)
        acc = acc + jnp.dot(xo_ref[rows, :], wo_ref[DIFF_WIDTH + SSM_WIDTH:, :],
                            preferred_element_type=F32)
        zg = z.astype(F32) * (1.0 / (1.0 + jnp.exp(-gate_in)))
        s_out = _rms(zg, gs_ref[...]).astype(BF16)
        acc = acc + jnp.dot(s_out, wo_ref[DIFF_WIDTH:DIFF_WIDTH + SSM_WIDTH, :],
                            preferred_element_type=F32)
        h = x_ref[rows, :] + acc
        h_out[rows, :] = h
        hm_out[rows, :] = _rms(h, gm_ref[...]).astype(BF16)


def _outproj(a_out, z, xo, x2d, w_glu, b_glu, g_ssm, w_out, g_mlp):
    rows, d = x2d.shape
    tm = ROW_TILE
    mix = w_out.shape[0]
    row = lambda i: (i, 0)
    const = lambda i: (0, 0)
    blk = (tm * DIFF_WIDTH * 2 + tm * SSM_WIDTH * 2 + tm * XATTN_WIDTH * 2 + tm * d * 4
           + SSM_WIDTH * SSM_WIDTH * 2 + mix * d * 2 + tm * d * 4 + tm * d * 2)
    return pl.pallas_call(
        _outproj_kernel,
        grid=(rows // tm,),
        in_specs=[
            pl.BlockSpec((tm, DIFF_WIDTH), row),
            pl.BlockSpec((SSM_TILES, tm // SSM_CHUNK, SSM_CHUNK * LANES), lambda i: (0, i, 0)),
            pl.BlockSpec((tm, XATTN_WIDTH), row),
            pl.BlockSpec((tm, d), row),
            pl.BlockSpec((SSM_WIDTH, SSM_WIDTH), const),
            pl.BlockSpec((1, SSM_WIDTH), const),
            pl.BlockSpec((1, SSM_WIDTH), const),
            pl.BlockSpec((mix, d), const),
            pl.BlockSpec((1, d), const),
        ],
        out_specs=[pl.BlockSpec((tm, d), row), pl.BlockSpec((tm, d), row)],
        out_shape=[jax.ShapeDtypeStruct((rows, d), F32), jax.ShapeDtypeStruct((rows, d), BF16)],
        scratch_shapes=[pltpu.VMEM((SSM_TILES, tm, LANES), F32)],
        compiler_params=pltpu.CompilerParams(
            dimension_semantics=("parallel",),
            vmem_limit_bytes=_vmem_limit(blk, 3 * tm * d * 4)),
        name="outproj",
    )(a_out, z, xo, x2d, w_glu, b_glu, g_ssm, w_out, g_mlp)


def _mlp_kernel(hm_ref, w1_ref, w2_ref, h_ref, o_ref):
    f = pl.program_id(1)

    @pl.when(f == 0)
    def _():
        o_ref[...] = h_ref[...]

    ff = jnp.dot(hm_ref[...], w1_ref[...], preferred_element_type=F32)
    ff = jnp.square(jnp.maximum(ff, 0.0)).astype(BF16)
    o_ref[...] += jnp.dot(ff, w2_ref[...], preferred_element_type=F32)


def _mlp(hm, h, w1, w2):
    rows, d = h.shape
    d_ff = w1.shape[1]
    tm, tf = ROW_TILE, MLP_FF_TILE
    blk = d * tf * 2 + tf * d * 2 + tm * d * 4
    single = tm * d * 2 + tm * d * 4
    return pl.pallas_call(
        _mlp_kernel,
        grid=(rows // tm, d_ff // tf),
        in_specs=[
            pl.BlockSpec((tm, d), lambda i, f: (i, 0), pipeline_mode=pl.Buffered(1)),
            pl.BlockSpec((d, tf), lambda i, f: (0, f)),
            pl.BlockSpec((tf, d), lambda i, f: (f, 0)),
            pl.BlockSpec((tm, d), lambda i, f: (i, 0), pipeline_mode=pl.Buffered(1)),
        ],
        out_specs=pl.BlockSpec((tm, d), lambda i, f: (i, 0)),
        out_shape=jax.ShapeDtypeStruct((rows, d), F32),
        compiler_params=pltpu.CompilerParams(
            dimension_semantics=("parallel", "arbitrary"),
            vmem_limit_bytes=_vmem_limit(blk, single + tm * tf * 6 + tm * d * 4)),
        name="mlp",
    )(hm, w1, w2, h)


def _rotary_tables(seq_len):
    dim = DIFF_QK_DIM
    inv_freq = 1.0 / (ROPE_THETA ** (jnp.arange(0, dim, 2, dtype=F32) / dim))
    freqs = jnp.arange(seq_len, dtype=F32)[:, None] * inv_freq[None, :]
    emb = jnp.concatenate([freqs, freqs, freqs, freqs], axis=-1)
    return jnp.cos(emb), jnp.sin(emb)


def kernel(x, mem, g_attn_norm, w_in, g_q, g_k, lam_q1, lam_k1, lam_q2, lam_k2, g_subln, lam_re, lam_im, log_dt, b_re, b_im, c_re, c_im, d_skip, w_glu, b_glu, g_ssm_out, g_mem, w_mem_kv, g_xq, g_xk, g_xattn_out, w_out, g_mlp_norm, w_mlp_in, w_mlp_out):
    b, s, d = x.shape
    m = mem.shape[1]
    depth = w_in.shape[0]
    rows = b * s
    cos2, sin2 = _rotary_tables(s)
    h = x.reshape(rows, d)
    mem2d = mem.reshape(b * m, d)
    row2 = lambda v: v.reshape(1, -1)
    for layer in range(depth):
        lambda_init = 0.8 - 0.6 * math.exp(-0.3 * layer)

        xk, xv = _memkv(mem2d, row2(g_mem[layer]), w_mem_kv[layer].astype(BF16),
                        row2(g_xk[layer]))
        q, k, vt, u, x_out = _inproj(
            h, s, row2(g_attn_norm[layer]), w_in[layer].astype(BF16),
            row2(jnp.tile(g_q[layer], 2)), row2(jnp.tile(g_k[layer], 2)), cos2, sin2,
            xk.reshape(b, m, XATTN_WIDTH), xv.reshape(b, m, XATTN_WIDTH),
            row2(g_xq[layer]), row2(g_xattn_out[layer]))

        a_out = _attention(q.reshape(b, s, DIFF_WIDTH), k.reshape(b, s, DIFF_WIDTH),
                           vt, row2(lam_q1[layer]), row2(lam_k1[layer]),
                           row2(lam_q2[layer]), row2(lam_k2[layer]),
                           g_subln[layer].reshape(DIFF_V_DIM, 1), lambda_init)

        m_op, w_op, n_op, a_pow = _ssm_prep(lam_re[layer], lam_im[layer], log_dt[layer],
                                            b_re[layer], b_im[layer], c_re[layer], c_im[layer],
                                            d_skip[layer])
        x_state = _ssm_scan(_ssm_sum(u, w_op), a_pow, b)
        z = _ssm_out(u, m_op, x_state, n_op)

        h, hm = _outproj(a_out.reshape(rows, DIFF_WIDTH), z, x_out, h,
                         w_glu[layer].astype(BF16), row2(b_glu[layer]), row2(g_ssm_out[layer]),
                         w_out[layer].astype(BF16), row2(g_mlp_norm[layer]))
        h = _mlp(hm, h, w_mlp_in[layer].astype(BF16), w_mlp_out[layer].astype(BF16))
    return h.reshape(b, s, d)
```

```python
import functools
import math

import jax
import jax.numpy as jnp
from jax import lax
from jax.experimental import pallas as pl
from jax.experimental.pallas import tpu as pltpu

F32 = jnp.float32
BF16 = jnp.bfloat16

LANES = 128
VMEM_CAP_BYTES = 60000 * 1024

DIFF_HEADS = 8
DIFF_QK_DIM = 64
DIFF_V_DIM = 2 * DIFF_QK_DIM
DIFF_WIDTH = DIFF_HEADS * DIFF_V_DIM
SSM_GROUPS = 32
SSM_GROUP_CH = 16
SSM_STATE = 64
SSM_WIDTH = SSM_GROUPS * SSM_GROUP_CH
XATTN_HEADS = 4
XATTN_HEAD_DIM = 128
XATTN_WIDTH = XATTN_HEADS * XATTN_HEAD_DIM
ROPE_THETA = 10000.0
EPS = 1e-6
LOG2_E = math.log2(math.e)

SSM_CHUNK = 16
SSM_LANE_GROUPS = LANES // SSM_GROUP_CH
SSM_TILES = SSM_WIDTH // LANES
SSM_TILE_STATE = SSM_LANE_GROUPS * SSM_STATE

ROW_TILE = 512
ATTN_TILE = 512
ATTN_KEY_TILE = 256
ATTN_ONES_ROWS = 16
MLP_FF_TILE = 1024

_NT = (((1,), (1,)), ((), ()))


def _vmem_limit(block_bytes, scratch_bytes=0):
    return int(min(2 * block_bytes + scratch_bytes + (4 << 20), VMEM_CAP_BYTES))


def _rms(t, g):
    ms = jnp.mean(t * t, axis=-1, keepdims=True)
    return t * lax.rsqrt(ms + EPS) * g


def _memkv_kernel(mem_ref, g_ref, w_ref, gxk_ref, k_out, v_out):
    hn = _rms(mem_ref[...], g_ref[...]).astype(BF16)
    kv = jnp.dot(hn, w_ref[...], preferred_element_type=F32)
    gxk = gxk_ref[...]
    for hh in range(XATTN_HEADS):
        sl = slice(hh * XATTN_HEAD_DIM, (hh + 1) * XATTN_HEAD_DIM)
        k_out[:, sl] = _rms(kv[:, sl], gxk).astype(BF16)
    v_out[...] = kv[:, XATTN_WIDTH:].astype(BF16)


def _memkv(mem2d, g_mem, w_kv, g_xk):
    rows, d = mem2d.shape
    blk = rows * d * 4 + d * 2 * XATTN_WIDTH * 2 + 2 * rows * XATTN_WIDTH * 2
    return pl.pallas_call(
        _memkv_kernel,
        grid=(1,),
        in_specs=[
            pl.BlockSpec((rows, d), lambda i: (0, 0)),
            pl.BlockSpec((1, d), lambda i: (0, 0)),
            pl.BlockSpec((d, 2 * XATTN_WIDTH), lambda i: (0, 0)),
            pl.BlockSpec((1, XATTN_HEAD_DIM), lambda i: (0, 0)),
        ],
        out_specs=[
            pl.BlockSpec((rows, XATTN_WIDTH), lambda i: (0, 0)),
            pl.BlockSpec((rows, XATTN_WIDTH), lambda i: (0, 0)),
        ],
        out_shape=[jax.ShapeDtypeStruct((rows, XATTN_WIDTH), BF16)] * 2,
        compiler_params=pltpu.CompilerParams(
            vmem_limit_bytes=_vmem_limit(blk, rows * 2 * XATTN_WIDTH * 8)),
        name="memkv",
    )(mem2d, g_mem, w_kv, g_xk)


def _inproj_kernel(x_ref, gn_ref, w_ref, gq_ref, gk_ref, cos_ref, sin_ref,
                   xk_ref, xv_ref, gxq_ref, gxo_ref,
                   q_out, k_out, v_out, u_out, xo_out, u_scr):
    hn = _rms(x_ref[...], gn_ref[...]).astype(BF16)
    tm = hn.shape[0]
    tn = DIFF_WIDTH

    def project(j):
        return jnp.dot(hn, w_ref[:, j * tn:(j + 1) * tn], preferred_element_type=F32)

    def qk_epilogue(acc, g_ref, out_ref, scale):
        lane = lax.broadcasted_iota(jnp.int32, (tm, LANES), 1)
        lo_comp = lane < DIFF_QK_DIM
        first_half = (lane & (DIFF_QK_DIM - 1)) < DIFF_QK_DIM // 2
        cos = cos_ref[...]
        sin = sin_ref[...]
        g = g_ref[...]
        for h in range(DIFF_HEADS):
            sl = slice(h * LANES, (h + 1) * LANES)
            t = acc[:, sl]
            t2 = t * t
            s_lo = jnp.sum(jnp.where(lo_comp, t2, 0.0), axis=-1, keepdims=True)
            s_hi = jnp.sum(jnp.where(lo_comp, 0.0, t2), axis=-1, keepdims=True)
            ms = jnp.where(lo_comp, s_lo, s_hi) * (1.0 / DIFF_QK_DIM)
            tn_ = t * lax.rsqrt(ms + EPS) * g
            fwd = pltpu.roll(tn_, DIFF_QK_DIM // 2, 1)
            bwd = pltpu.roll(tn_, LANES - DIFF_QK_DIM // 2, 1)
            rot = jnp.where(first_half, -bwd, fwd)
            out_ref[:, sl] = ((tn_ * cos + rot * sin) * scale).astype(BF16)

    qk_epilogue(project(0), gq_ref, q_out, DIFF_QK_DIM ** -0.5 * LOG2_E)
    qk_epilogue(project(1), gk_ref, k_out, 1.0)
    v_out[0] = project(2).T.astype(BF16)

    acc = project(3)
    for c in range(SSM_TILES):
        u_scr[c] = acc[:, c * LANES:(c + 1) * LANES]
        for t in range(SSM_CHUNK):
            u_out[c, :, t * LANES:(t + 1) * LANES] = (
                u_scr[c, pl.ds(t, tm // SSM_CHUNK, stride=SSM_CHUNK), :].astype(BF16))
    gxq = gxq_ref[...]
    outs = []
    ssq = jnp.zeros((tm, 1), F32)
    for hh in range(XATTN_HEADS):
        sl = slice(hh * XATTN_HEAD_DIM, (hh + 1) * XATTN_HEAD_DIM)
        t = acc[:, SSM_WIDTH + hh * XATTN_HEAD_DIM:SSM_WIDTH + (hh + 1) * XATTN_HEAD_DIM]
        qn = (_rms(t, gxq) * (XATTN_HEAD_DIM ** -0.5)).astype(BF16)
        s = lax.dot_general(qn, xk_ref[0, :, sl], _NT, preferred_element_type=F32)
        p = jnp.exp(s - jnp.max(s, axis=-1, keepdims=True))
        l = jnp.sum(p, axis=-1, keepdims=True)
        o = jnp.dot(p.astype(BF16), xv_ref[0, :, sl], preferred_element_type=F32) / l
        outs.append(o)
        ssq = ssq + jnp.sum(o * o, axis=-1, keepdims=True)
    inv = lax.rsqrt(ssq * (1.0 / XATTN_WIDTH) + EPS)
    for hh in range(XATTN_HEADS):
        sl = slice(hh * XATTN_HEAD_DIM, (hh + 1) * XATTN_HEAD_DIM)
        xo_out[:, sl] = (outs[hh] * inv * gxo_ref[:, sl]).astype(BF16)


def _inproj(x2d, seq, g_norm, w_in, g_q2, g_k2, cos2, sin2, xk, xv, g_xq, g_xo):
    rows, d = x2d.shape
    tm = ROW_TILE
    tn = DIFF_WIDTH
    n_in = w_in.shape[1]
    tiles_per_seq = seq // tm
    mem_len = xk.shape[1]
    blk = (tm * d * 4 + 2 * tm * LANES * 4 + 2 * mem_len * XATTN_WIDTH * 2
           + 3 * tm * tn * 2 + tm * SSM_WIDTH * 2 + tm * XATTN_WIDTH * 2)
    row = lambda i: (i, 0)
    const = lambda i: (0, 0)
    return pl.pallas_call(
        _inproj_kernel,
        grid=(rows // tm,),
        in_specs=[
            pl.BlockSpec((tm, d), row),
            pl.BlockSpec((1, d), const),
            pl.BlockSpec((d, n_in), const, pipeline_mode=pl.Buffered(1)),
            pl.BlockSpec((1, LANES), const),
            pl.BlockSpec((1, LANES), const),
            pl.BlockSpec((tm, LANES), lambda i: (i % tiles_per_seq, 0)),
            pl.BlockSpec((tm, LANES), lambda i: (i % tiles_per_seq, 0)),
            pl.BlockSpec((1, mem_len, XATTN_WIDTH), lambda i: (i // tiles_per_seq, 0, 0)),
            pl.BlockSpec((1, mem_len, XATTN_WIDTH), lambda i: (i // tiles_per_seq, 0, 0)),
            pl.BlockSpec((1, XATTN_HEAD_DIM), const),
            pl.BlockSpec((1, XATTN_WIDTH), const),
        ],
        out_specs=[
            pl.BlockSpec((tm, tn), row),
            pl.BlockSpec((tm, tn), row),
            pl.BlockSpec((1, tn, tm), lambda i: (i // tiles_per_seq, 0, i % tiles_per_seq)),
            pl.BlockSpec((SSM_TILES, tm // SSM_CHUNK, SSM_CHUNK * LANES), lambda i: (0, i, 0)),
            pl.BlockSpec((tm, XATTN_WIDTH), row),
        ],
        out_shape=[
            jax.ShapeDtypeStruct((rows, tn), BF16),
            jax.ShapeDtypeStruct((rows, tn), BF16),
            jax.ShapeDtypeStruct((rows // seq, tn, seq), BF16),
            jax.ShapeDtypeStruct((SSM_TILES, rows // SSM_CHUNK, SSM_CHUNK * LANES), BF16),
            jax.ShapeDtypeStruct((rows, XATTN_WIDTH), BF16),
        ],
        scratch_shapes=[pltpu.VMEM((SSM_TILES, tm, LANES), F32)],
        compiler_params=pltpu.CompilerParams(
            dimension_semantics=("parallel",),
            vmem_limit_bytes=_vmem_limit(blk, d * n_in * 2 + tm * d * 2 + 6 * tm * tn * 4)),
        name="inproj",
    )(x2d, g_norm, w_in, g_q2, g_k2, cos2, sin2, xk, xv, g_xq, g_xo)


def _attn_kernel(lq1_ref, lk1_ref, lq2_ref, lk2_ref, gsub_ref, q_ref, k_ref, vt_ref, o_ref,
                 s_scr, acc_scr, *, lambda_init):
    qi = pl.program_id(2)
    tq = q_ref.shape[1]
    tk = ATTN_KEY_TILE
    blocks_per_q = tq // tk
    q_t = q_ref[0].astype(F32).T.astype(BF16)
    feat = lax.broadcasted_iota(jnp.int32, (LANES, tq), 0)
    zero = jnp.zeros_like(q_t)
    qs = (jnp.where(feat < DIFF_QK_DIM, q_t, zero), jnp.where(feat >= DIFF_QK_DIM, q_t, zero))

    acc_scr[...] = jnp.zeros(acc_scr.shape, F32)

    def scores(kb, slot):
        k = k_ref[0, pl.ds(pl.multiple_of(kb * tk, tk), tk), :]
        for c in range(2):
            s_scr[slot, c] = jnp.dot(k, qs[c], preferred_element_type=F32)

    def softmax_pv(kb, slot, stats, diag_index=None):
        vt = vt_ref[0, :, pl.ds(pl.multiple_of(kb * tk, tk), tk)]
        vt_ones = jnp.concatenate([vt, jnp.ones((ATTN_ONES_ROWS, tk), BF16)], axis=0)
        new_stats = []
        for c in range(2):
            m_old = stats[c]
            s = s_scr[slot, c]
            if diag_index is not None:
                key = lax.broadcasted_iota(jnp.int32, (tk, tq), 0) + diag_index * tk
                qry = lax.broadcasted_iota(jnp.int32, (tk, tq), 1)
                s = jnp.where(key <= qry, s, -jnp.inf)
            m_new = jnp.maximum(m_old, jnp.max(s, axis=0, keepdims=True))
            alpha = jnp.exp2(m_old - m_new)
            p = jnp.exp2((s - m_new).astype(BF16))
            acc_scr[c] = alpha * acc_scr[c] + jnp.dot(vt_ones, p, preferred_element_type=F32)
            new_stats.append(m_new)
        return tuple(new_stats)

    def pair(i, stats):
        kb = blocks_per_q * i
        for j in range(blocks_per_q):
            scores(kb + j + 1, (j + 1) % 2)
            stats = softmax_pv(kb + j, j % 2, stats)
        return stats

    init = (jnp.full((1, tq), -jnp.inf, F32),) * 2
    def double_pair(i2, stats):
        return pair(2 * i2 + 1, pair(2 * i2, stats))

    scores(0, 0)
    stats = lax.fori_loop(0, qi // 2, double_pair, init)
    stats = lax.cond(qi % 2 == 1, lambda st: pair(qi - 1, st), lambda st: st, stats)
    kb = blocks_per_q * qi
    for j in range(blocks_per_q):
        if j + 1 < blocks_per_q:
            scores(kb + j + 1, (j + 1) % 2)
        stats = softmax_pv(kb + j, j % 2, stats, diag_index=j)

    lam = (jnp.exp(jnp.sum(lq1_ref[...] * lk1_ref[...], axis=-1, keepdims=True))
           - jnp.exp(jnp.sum(lq2_ref[...] * lk2_ref[...], axis=-1, keepdims=True))
           + lambda_init)
    nv = DIFF_V_DIM
    o_t = (acc_scr[0, 0:nv] / acc_scr[0, nv:nv + 1]
           - lam * (acc_scr[1, 0:nv] / acc_scr[1, nv:nv + 1]))
    ms = jnp.mean(o_t * o_t, axis=0, keepdims=True)
    o_t = o_t * lax.rsqrt(ms + EPS) * (gsub_ref[...] * (1.0 - lambda_init))
    o_ref[0] = o_t.T.astype(BF16)


def _attention(q, k, vt, lq1, lk1, lq2, lk2, g_subln_col, lambda_init):
    b, s, _ = q.shape
    tq = ATTN_TILE
    tk = ATTN_KEY_TILE
    assert tq % (2 * tk) == 0
    vec = pl.BlockSpec((1, DIFF_QK_DIM), lambda bi, h, qi: (0, 0))
    blk = 2 * tq * LANES * 2 + 2 * s * LANES * 2
    return pl.pallas_call(
        functools.partial(_attn_kernel, lambda_init=lambda_init),
        grid=(b, DIFF_HEADS, s // tq),
        in_specs=[
            vec, vec, vec, vec,
            pl.BlockSpec((DIFF_V_DIM, 1), lambda bi, h, qi: (0, 0)),
            pl.BlockSpec((1, tq, LANES), lambda bi, h, qi: (bi, qi, h)),
            pl.BlockSpec((1, s, LANES), lambda bi, h, qi: (bi, 0, h)),
            pl.BlockSpec((1, DIFF_V_DIM, s), lambda bi, h, qi: (bi, h, 0)),
        ],
        out_specs=pl.BlockSpec((1, tq, LANES), lambda bi, h, qi: (bi, qi, h)),
        out_shape=jax.ShapeDtypeStruct((b, s, DIFF_WIDTH), BF16),
        scratch_shapes=[
            pltpu.VMEM((2, 2, tk, tq), F32),
            pltpu.VMEM((2, DIFF_V_DIM + ATTN_ONES_ROWS, tq), F32),
        ],
        compiler_params=pltpu.CompilerParams(
            dimension_semantics=("parallel", "parallel", "arbitrary"),
            vmem_limit_bytes=_vmem_limit(blk, 2 * DIFF_V_DIM * tq * 4 + 12 * tk * tq * 4)),
        name="diffattn",
    )(lq1, lk1, lq2, lk2, g_subln_col, q, k, vt)


def _ssm_prep_kernel(lr_row, li_row, ld_row, lr_col, li_col, ld_col,
                     bre_ref, bim_ref, cre_ref, cim_ref, d_ref,
                     m_out, w_out, n_out, a_out):
    t_len = SSM_CHUNK
    ns = SSM_TILE_STATE

    def lam_bar(lr, li, ld):
        dt = jnp.exp(ld)
        e = jnp.exp(lr * dt)
        return e * jnp.cos(li * dt), e * jnp.sin(li * dt)

    lam_r = lr_row[0]
    lam_i = li_row[0]
    br, bi = lam_bar(lam_r, lam_i, ld_row[0])
    den = lam_r * lam_r + lam_i * lam_i
    nr = br - 1.0
    coef_r = (nr * lam_r + bi * lam_i) / den
    coef_i = (bi * lam_r - nr * lam_i) / den
    b_r = bre_ref[0]
    b_i = bim_ref[0]
    bb_r = coef_r * b_r - coef_i * b_i
    bb_i = coef_r * b_i + coef_i * b_r
    pr = jnp.ones_like(br)
    pi = jnp.zeros_like(br)
    for tau in range(t_len):
        rows = slice((t_len - 1 - tau) * LANES, (t_len - tau) * LANES)
        w_out[0, rows, 0:ns] = (bb_r * pr - bb_i * pi).astype(BF16)
        w_out[0, rows, ns:2 * ns] = (bb_r * pi + bb_i * pr).astype(BF16)
        pr, pi = pr * br - pi * bi, pr * bi + pi * br
    a_out[0, :, 0:ns] = pr
    a_out[0, :, ns:2 * ns] = pi

    cr, ci = lam_bar(lr_col[0], li_col[0], ld_col[0])
    c_r = cre_ref[0]
    c_i = cim_ref[0]
    qr = jnp.ones_like(cr)
    qi = jnp.zeros_like(cr)
    eye = (lax.broadcasted_iota(jnp.int32, (LANES, LANES), 0)
           == lax.broadcasted_iota(jnp.int32, (LANES, LANES), 1))
    kblk = []
    for tau in range(t_len + 1):
        cl_r = c_r * qr - c_i * qi
        cl_i = c_r * qi + c_i * qr
        if tau < t_len:
            kb = (jnp.dot(bb_r.astype(BF16), cl_r.astype(BF16), preferred_element_type=F32)
                  - jnp.dot(bb_i.astype(BF16), cl_i.astype(BF16), preferred_element_type=F32))
            if tau == 0:
                kb = kb + jnp.where(eye, d_ref[0], 0.0)
            kblk.append(kb.astype(BF16))
        if tau >= 1:
            cols = slice((tau - 1) * LANES, tau * LANES)
            n_out[0, 0:ns, cols] = cl_r.astype(BF16)
            n_out[0, ns:2 * ns, cols] = (-cl_i).astype(BF16)
        qr, qi = qr * cr - qi * ci, qr * ci + qi * cr
    zeros = jnp.zeros((LANES, LANES), BF16)
    for s in range(t_len):
        for t in range(t_len):
            m_out[0, s * LANES:(s + 1) * LANES, t * LANES:(t + 1) * LANES] = (
                kblk[t - s] if t >= s else zeros)


def _ssm_prep(lam_re, lam_im, log_dt, b_re, b_im, c_re, c_im, d_skip):
    nj, gj, p, h = SSM_TILES, SSM_LANE_GROUPS, SSM_STATE, SSM_GROUP_CH
    ns, tl = SSM_TILE_STATE, SSM_CHUNK * LANES
    eye = jnp.eye(gj, dtype=F32)
    dt_full = jnp.broadcast_to(log_dt[:, None], (SSM_GROUPS, p))

    def expand_b(b):
        b4 = b.reshape(nj, gj, p, h).transpose(0, 1, 3, 2)
        return (b4[:, :, :, None, :] * eye[None, :, None, :, None]).reshape(nj, LANES, ns)

    def expand_c(c):
        c4 = c.reshape(nj, gj, h, p).transpose(0, 1, 3, 2)
        return (c4[:, :, :, None, :] * eye[None, :, None, :, None]).reshape(nj, ns, LANES)

    args = (lam_re.reshape(nj, 1, ns), lam_im.reshape(nj, 1, ns), dt_full.reshape(nj, 1, ns),
            lam_re.reshape(nj, ns, 1), lam_im.reshape(nj, ns, 1), dt_full.reshape(nj, ns, 1),
            expand_b(b_re), expand_b(b_im), expand_c(c_re), expand_c(c_im),
            d_skip.reshape(nj, 1, LANES))
    spec = lambda shp: pl.BlockSpec((1,) + shp, lambda j: (j, 0, 0))
    out_bytes = tl * tl * 2 + 2 * tl * 2 * ns * 2 + 2 * ns * 4
    return pl.pallas_call(
        _ssm_prep_kernel,
        grid=(nj,),
        in_specs=[spec((1, ns))] * 3 + [spec((ns, 1))] * 3 + [spec((LANES, ns))] * 2
                 + [spec((ns, LANES))] * 2 + [spec((1, LANES))],
        out_specs=[spec((tl, tl)), spec((tl, 2 * ns)), spec((2 * ns, tl)), spec((1, 2 * ns))],
        out_shape=[
            jax.ShapeDtypeStruct((nj, tl, tl), BF16),
            jax.ShapeDtypeStruct((nj, tl, 2 * ns), BF16),
            jax.ShapeDtypeStruct((nj, 2 * ns, tl), BF16),
            jax.ShapeDtypeStruct((nj, 1, 2 * ns), F32),
        ],
        compiler_params=pltpu.CompilerParams(
            dimension_semantics=("parallel",),
            vmem_limit_bytes=_vmem_limit(out_bytes + 7 * LANES * ns * 4, 8 << 20)),
        name="ssm_prep",
    )(*args)


def _ssm_sum_kernel(u_ref, w_ref, s_out):
    s_out[0] = jnp.dot(u_ref[0], w_ref[0], preferred_element_type=F32)


def _ssm_sum(u2, w_op):
    nj, rows, tl = u2.shape
    ns2 = w_op.shape[2]
    spec = lambda shp: pl.BlockSpec((1,) + shp, lambda j: (j, 0, 0))
    blk = rows * tl * 2 + tl * ns2 * 2 + rows * ns2 * 4
    return pl.pallas_call(
        _ssm_sum_kernel,
        grid=(nj,),
        in_specs=[spec((rows, tl)), spec((tl, ns2))],
        out_specs=spec((rows, ns2)),
        out_shape=jax.ShapeDtypeStruct((nj, rows, ns2), F32),
        compiler_params=pltpu.CompilerParams(
            dimension_semantics=("parallel",), vmem_limit_bytes=_vmem_limit(blk)),
        name="ssm_sum",
    )(u2, w_op)


def _ssm_scan_kernel(s_ref, a_ref, x_out, *, n_batch, n_chunk):
    ns = SSM_TILE_STATE
    a = a_ref[0]
    ar = a[:, 0:ns]
    ai = a[:, ns:2 * ns]

    def body(c, carry):
        new = []
        for b in range(n_batch):
            xr, xi = carry[2 * b], carry[2 * b + 1]
            r = b * n_chunk + c
            x_out[0, pl.ds(r, 1), 0:ns] = xr
            x_out[0, pl.ds(r, 1), ns:2 * ns] = xi
            s = s_ref[0, pl.ds(r, 1), :]
            new.append(ar * xr - ai * xi + s[:, 0:ns])
            new.append(ar * xi + ai * xr + s[:, ns:2 * ns])
        return tuple(new)

    init = tuple(jnp.zeros((1, ns), F32) for _ in range(2 * n_batch))
    lax.fori_loop(0, n_chunk, body, init)


def _ssm_scan(s_sum, a_pow, n_batch):
    nj, rows, ns2 = s_sum.shape
    spec = lambda shp: pl.BlockSpec((1,) + shp, lambda j: (j, 0, 0))
    return pl.pallas_call(
        functools.partial(_ssm_scan_kernel, n_batch=n_batch, n_chunk=rows // n_batch),
        grid=(nj,),
        in_specs=[spec((rows, ns2)), spec((1, ns2))],
        out_specs=spec((rows, ns2)),
        out_shape=jax.ShapeDtypeStruct((nj, rows, ns2), F32),
        compiler_params=pltpu.CompilerParams(
            dimension_semantics=("parallel",),
            vmem_limit_bytes=_vmem_limit(2 * rows * ns2 * 4)),
        name="ssm_scan",
    )(s_sum, a_pow)


def _gelu_tanh(x):
    return x * (0.5 * (1.0 + jnp.tanh(math.sqrt(2.0 / math.pi) * (x + 0.044715 * (x * x * x)))))


def _ssm_out_kernel(u_ref, m_ref, x_ref, n_ref, z_out):
    xb = x_ref[0].astype(BF16)
    wide = 2 * LANES
    for n in range(SSM_CHUNK // 2):
        cols = slice(n * wide, (n + 1) * wide)
        depth = (n + 1) * wide
        y = jnp.dot(u_ref[0, :, 0:depth], m_ref[0, 0:depth, cols], preferred_element_type=F32)
        y = y + jnp.dot(xb, n_ref[0, :, cols], preferred_element_type=F32)
        z_out[0, :, cols] = _gelu_tanh(y).astype(BF16)


def _ssm_out(u2, m_op, x_state, n_op):
    nj, rows, tl = u2.shape
    ns2 = x_state.shape[2]
    n_split = 2
    tr = rows // n_split
    rspec = lambda shp: pl.BlockSpec((1,) + shp, lambda j, r: (j, r, 0))
    cspec = lambda shp: pl.BlockSpec((1,) + shp, lambda j, r: (j, 0, 0))
    blk = tr * tl * 2 + tl * tl * 2 + tr * ns2 * 4 + ns2 * tl * 2 + tr * tl * 2
    return pl.pallas_call(
        _ssm_out_kernel,
        grid=(nj, n_split),
        in_specs=[rspec((tr, tl)), cspec((tl, tl)), rspec((tr, ns2)), cspec((ns2, tl))],
        out_specs=rspec((tr, tl)),
        out_shape=jax.ShapeDtypeStruct((nj, rows, tl), BF16),
        compiler_params=pltpu.CompilerParams(
            dimension_semantics=("parallel", "arbitrary"),
            vmem_limit_bytes=_vmem_limit(blk, 3 * tr * tl * 4)),
        name="ssm_out",
    )(u2, m_op, x_state, n_op)


def _outproj_kernel(a_ref, z_ref, xo_ref, x_ref, wglu_ref, bglu_ref, gs_ref, wo_ref, gm_ref,
                    h_out, hm_out, z_scr):
    tm = x_ref.shape[0]
    half = tm // 2
    halves = [slice(r * half, (r + 1) * half) for r in range(2)]
    for c in range(SSM_TILES):
        for t in range(SSM_CHUNK):
            z_scr[c, pl.ds(t, tm // SSM_CHUNK, stride=SSM_CHUNK), :] = (
                z_ref[c, :, t * LANES:(t + 1) * LANES].astype(F32))
    zs = [jnp.concatenate([z_scr[c, rows, :] for c in range(SSM_TILES)], axis=-1).astype(BF16)
          for rows in halves]
    gates = [jnp.dot(z, wglu_ref[...], preferred_element_type=F32) + bglu_ref[...] for z in zs]
    for rows, z, gate_in in zip(halves, zs, gates):
        acc = jnp.dot(a_ref[rows, :], wo_ref[0:DIFF_WIDTH, :], preferred_element_type=F32)
        acc = acc + jnp.dot(xo_ref[rows, :], wo_ref[DIFF_WIDTH + SSM_WIDTH:, :],
                            preferred_element_type=F32)
        zg = z.astype(F32) * (1.0 / (1.0 + jnp.exp(-gate_in)))
        s_out = _rms(zg, gs_ref[...]).astype(BF16)
        acc = acc + jnp.dot(s_out, wo_ref[DIFF_WIDTH:DIFF_WIDTH + SSM_WIDTH, :],
                            preferred_element_type=F32)
        h = x_ref[rows, :] + acc
        h_out[rows, :] = h
        hm_out[rows, :] = _rms(h, gm_ref[...]).astype(BF16)


def _outproj(a_out, z, xo, x2d, w_glu, b_glu, g_ssm, w_out, g_mlp):
    rows, d = x2d.shape
    tm = ROW_TILE
    mix = w_out.shape[0]
    row = lambda i: (i, 0)
    const = lambda i: (0, 0)
    blk = (tm * DIFF_WIDTH * 2 + tm * SSM_WIDTH * 2 + tm * XATTN_WIDTH * 2 + tm * d * 4
           + SSM_WIDTH * SSM_WIDTH * 2 + mix * d * 2 + tm * d * 4 + tm * d * 2)
    return pl.pallas_call(
        _outproj_kernel,
        grid=(rows // tm,),
        in_specs=[
            pl.BlockSpec((tm, DIFF_WIDTH), row),
            pl.BlockSpec((SSM_TILES, tm // SSM_CHUNK, SSM_CHUNK * LANES), lambda i: (0, i, 0)),
            pl.BlockSpec((tm, XATTN_WIDTH), row),
            pl.BlockSpec((tm, d), row),
            pl.BlockSpec((SSM_WIDTH, SSM_WIDTH), const),
            pl.BlockSpec((1, SSM_WIDTH), const),
            pl.BlockSpec((1, SSM_WIDTH), const),
            pl.BlockSpec((mix, d), const),
            pl.BlockSpec((1, d), const),
        ],
        out_specs=[pl.BlockSpec((tm, d), row), pl.BlockSpec((tm, d), row)],
        out_shape=[jax.ShapeDtypeStruct((rows, d), F32), jax.ShapeDtypeStruct((rows, d), BF16)],
        scratch_shapes=[pltpu.VMEM((SSM_TILES, tm, LANES), F32)],
        compiler_params=pltpu.CompilerParams(
            dimension_semantics=("parallel",),
            vmem_limit_bytes=_vmem_limit(blk, 3 * tm * d * 4)),
        name="outproj",
    )(a_out, z, xo, x2d, w_glu, b_glu, g_ssm, w_out, g_mlp)


def _mlp_kernel(hm_ref, w1_ref, w2_ref, h_ref, o_ref):
    f = pl.program_id(1)

    @pl.when(f == 0)
    def _():
        o_ref[...] = h_ref[...]

    ff = jnp.dot(hm_ref[...], w1_ref[...], preferred_element_type=F32)
    ff = jnp.square(jnp.maximum(ff, 0.0)).astype(BF16)
    o_ref[...] += jnp.dot(ff, w2_ref[...], preferred_element_type=F32)


def _mlp(hm, h, w1, w2):
    rows, d = h.shape
    d_ff = w1.shape[1]
    tm, tf = ROW_TILE, MLP_FF_TILE
    blk = tm * d * 2 + d * tf * 2 + tf * d * 2 + tm * d * 4 + tm * d * 4
    return pl.pallas_call(
        _mlp_kernel,
        grid=(rows // tm, d_ff // tf),
        in_specs=[
            pl.BlockSpec((tm, d), lambda i, f: (i, 0)),
            pl.BlockSpec((d, tf), lambda i, f: (0, f)),
            pl.BlockSpec((tf, d), lambda i, f: (f, 0)),
            pl.BlockSpec((tm, d), lambda i, f: (i, 0)),
        ],
        out_specs=pl.BlockSpec((tm, d), lambda i, f: (i, 0)),
        out_shape=jax.ShapeDtypeStruct((rows, d), F32),
        compiler_params=pltpu.CompilerParams(
            dimension_semantics=("parallel", "arbitrary"),
            vmem_limit_bytes=_vmem_limit(blk, tm * tf * 6 + tm * d * 4)),
        name="mlp",
    )(hm, w1, w2, h)


def _rotary_tables(seq_len):
    dim = DIFF_QK_DIM
    inv_freq = 1.0 / (ROPE_THETA ** (jnp.arange(0, dim, 2, dtype=F32) / dim))
    freqs = jnp.arange(seq_len, dtype=F32)[:, None] * inv_freq[None, :]
    emb = jnp.concatenate([freqs, freqs, freqs, freqs], axis=-1)
    return jnp.cos(emb), jnp.sin(emb)


def kernel(x, mem, g_attn_norm, w_in, g_q, g_k, lam_q1, lam_k1, lam_q2, lam_k2, g_subln, lam_re, lam_im, log_dt, b_re, b_im, c_re, c_im, d_skip, w_glu, b_glu, g_ssm_out, g_mem, w_mem_kv, g_xq, g_xk, g_xattn_out, w_out, g_mlp_norm, w_mlp_in, w_mlp_out):
    b, s, d = x.shape
    m = mem.shape[1]
    depth = w_in.shape[0]
    rows = b * s
    cos2, sin2 = _rotary_tables(s)
    h = x.reshape(rows, d)
    mem2d = mem.reshape(b * m, d)
    row2 = lambda v: v.reshape(1, -1)
    for layer in range(depth):
        lambda_init = 0.8 - 0.6 * math.exp(-0.3 * layer)

        xk, xv = _memkv(mem2d, row2(g_mem[layer]), w_mem_kv[layer].astype(BF16),
                        row2(g_xk[layer]))
        q, k, vt, u, x_out = _inproj(
            h, s, row2(g_attn_norm[layer]), w_in[layer].astype(BF16),
            row2(jnp.tile(g_q[layer], 2)), row2(jnp.tile(g_k[layer], 2)), cos2, sin2,
            xk.reshape(b, m, XATTN_WIDTH), xv.reshape(b, m, XATTN_WIDTH),
            row2(g_xq[layer]), row2(g_xattn_out[layer]))

        a_out = _attention(q.reshape(b, s, DIFF_WIDTH), k.reshape(b, s, DIFF_WIDTH),
                           vt, row2(lam_q1[layer]), row2(lam_k1[layer]),
                           row2(lam_q2[layer]), row2(lam_k2[layer]),
                           g_subln[layer].reshape(DIFF_V_DIM, 1), lambda_init)

        m_op, w_op, n_op, a_pow = _ssm_prep(lam_re[layer], lam_im[layer], log_dt[layer],
                                            b_re[layer], b_im[layer], c_re[layer], c_im[layer],
                                            d_skip[layer])
        x_state = _ssm_scan(_ssm_sum(u, w_op), a_pow, b)
        z = _ssm_out(u, m_op, x_state, n_op)

        h, hm = _outproj(a_out.reshape(rows, DIFF_WIDTH), z, x_out, h,
                         w_glu[layer].astype(BF16), row2(b_glu[layer]), row2(g_ssm_out[layer]),
                         w_out[layer].astype(BF16), row2(g_mlp_norm[layer]))
        h = _mlp(hm, h, w_mlp_in[layer].astype(BF16), w_mlp_out[layer].astype(BF16))
    return h.reshape(b, s, d)
```

```python
import functools
import math

import jax
import jax.numpy as jnp
from jax import lax
from jax.experimental import pallas as pl
from jax.experimental.pallas import tpu as pltpu

F32 = jnp.float32
BF16 = jnp.bfloat16

LANES = 128
VMEM_CAP_BYTES = 60000 * 1024

DIFF_HEADS = 8
DIFF_QK_DIM = 64
DIFF_V_DIM = 2 * DIFF_QK_DIM
DIFF_WIDTH = DIFF_HEADS * DIFF_V_DIM
SSM_GROUPS = 32
SSM_GROUP_CH = 16
SSM_STATE = 64
SSM_WIDTH = SSM_GROUPS * SSM_GROUP_CH
XATTN_HEADS = 4
XATTN_HEAD_DIM = 128
XATTN_WIDTH = XATTN_HEADS * XATTN_HEAD_DIM
ROPE_THETA = 10000.0
EPS = 1e-6
LOG2_E = math.log2(math.e)

SSM_CHUNK = 16
SSM_LANE_GROUPS = LANES // SSM_GROUP_CH
SSM_TILES = SSM_WIDTH // LANES
SSM_TILE_STATE = SSM_LANE_GROUPS * SSM_STATE

ROW_TILE = 512
ATTN_TILE = 1024
ATTN_KEY_TILE = 256
ATTN_ONES_ROWS = 16
MLP_FF_TILE = 1024

_NT = (((1,), (1,)), ((), ()))


def _vmem_limit(block_bytes, scratch_bytes=0):
    return int(min(2 * block_bytes + scratch_bytes + (4 << 20), VMEM_CAP_BYTES))


def _rms(t, g):
    ms = jnp.mean(t * t, axis=-1, keepdims=True)
    return t * lax.rsqrt(ms + EPS) * g


def _memkv_kernel(mem_ref, g_ref, w_ref, gxk_ref, k_out, v_out):
    hn = _rms(mem_ref[...], g_ref[...]).astype(BF16)
    kv = jnp.dot(hn, w_ref[...], preferred_element_type=F32)
    gxk = gxk_ref[...]
    for hh in range(XATTN_HEADS):
        sl = slice(hh * XATTN_HEAD_DIM, (hh + 1) * XATTN_HEAD_DIM)
        k_out[:, sl] = _rms(kv[:, sl], gxk).astype(BF16)
    v_out[...] = kv[:, XATTN_WIDTH:].astype(BF16)


def _memkv(mem2d, g_mem, w_kv, g_xk):
    rows, d = mem2d.shape
    blk = rows * d * 4 + d * 2 * XATTN_WIDTH * 2 + 2 * rows * XATTN_WIDTH * 2
    return pl.pallas_call(
        _memkv_kernel,
        grid=(1,),
        in_specs=[
            pl.BlockSpec((rows, d), lambda i: (0, 0)),
            pl.BlockSpec((1, d), lambda i: (0, 0)),
            pl.BlockSpec((d, 2 * XATTN_WIDTH), lambda i: (0, 0)),
            pl.BlockSpec((1, XATTN_HEAD_DIM), lambda i: (0, 0)),
        ],
        out_specs=[
            pl.BlockSpec((rows, XATTN_WIDTH), lambda i: (0, 0)),
            pl.BlockSpec((rows, XATTN_WIDTH), lambda i: (0, 0)),
        ],
        out_shape=[jax.ShapeDtypeStruct((rows, XATTN_WIDTH), BF16)] * 2,
        compiler_params=pltpu.CompilerParams(
            vmem_limit_bytes=_vmem_limit(blk, rows * 2 * XATTN_WIDTH * 8)),
        name="memkv",
    )(mem2d, g_mem, w_kv, g_xk)


def _inproj_kernel(x_ref, gn_ref, w_ref, gq_ref, gk_ref, cos_ref, sin_ref,
                   xk_ref, xv_ref, gxq_ref, gxo_ref,
                   q_out, k_out, v_out, u_out, xo_out, u_scr):
    hn = _rms(x_ref[...], gn_ref[...]).astype(BF16)
    tm = hn.shape[0]
    tn = DIFF_WIDTH

    def project(j):
        return jnp.dot(hn, w_ref[:, j * tn:(j + 1) * tn], preferred_element_type=F32)

    def qk_epilogue(acc, g_ref, out_ref, scale):
        lane = lax.broadcasted_iota(jnp.int32, (tm, LANES), 1)
        lo_comp = lane < DIFF_QK_DIM
        first_half = (lane & (DIFF_QK_DIM - 1)) < DIFF_QK_DIM // 2
        cos = cos_ref[...]
        sin = sin_ref[...]
        g = g_ref[...]
        for h in range(DIFF_HEADS):
            sl = slice(h * LANES, (h + 1) * LANES)
            t = acc[:, sl]
            t2 = t * t
            s_lo = jnp.sum(jnp.where(lo_comp, t2, 0.0), axis=-1, keepdims=True)
            s_hi = jnp.sum(jnp.where(lo_comp, 0.0, t2), axis=-1, keepdims=True)
            ms = jnp.where(lo_comp, s_lo, s_hi) * (1.0 / DIFF_QK_DIM)
            tn_ = t * lax.rsqrt(ms + EPS) * g
            fwd = pltpu.roll(tn_, DIFF_QK_DIM // 2, 1)
            bwd = pltpu.roll(tn_, LANES - DIFF_QK_DIM // 2, 1)
            rot = jnp.where(first_half, -bwd, fwd)
            out_ref[:, sl] = ((tn_ * cos + rot * sin) * scale).astype(BF16)

    qk_epilogue(project(0), gq_ref, q_out, DIFF_QK_DIM ** -0.5 * LOG2_E)
    qk_epilogue(project(1), gk_ref, k_out, 1.0)
    v_out[0] = project(2).T.astype(BF16)

    acc = project(3)
    for c in range(SSM_TILES):
        u_scr[c] = acc[:, c * LANES:(c + 1) * LANES]
        for t in range(SSM_CHUNK):
            u_out[c, :, t * LANES:(t + 1) * LANES] = (
                u_scr[c, pl.ds(t, tm // SSM_CHUNK, stride=SSM_CHUNK), :].astype(BF16))
    gxq = gxq_ref[...]
    outs = []
    ssq = jnp.zeros((tm, 1), F32)
    for hh in range(XATTN_HEADS):
        sl = slice(hh * XATTN_HEAD_DIM, (hh + 1) * XATTN_HEAD_DIM)
        t = acc[:, SSM_WIDTH + hh * XATTN_HEAD_DIM:SSM_WIDTH + (hh + 1) * XATTN_HEAD_DIM]
        qn = (_rms(t, gxq) * (XATTN_HEAD_DIM ** -0.5)).astype(BF16)
        s = lax.dot_general(qn, xk_ref[0, :, sl], _NT, preferred_element_type=F32)
        p = jnp.exp(s - jnp.max(s, axis=-1, keepdims=True))
        l = jnp.sum(p, axis=-1, keepdims=True)
        o = jnp.dot(p.astype(BF16), xv_ref[0, :, sl], preferred_element_type=F32) / l
        outs.append(o)
        ssq = ssq + jnp.sum(o * o, axis=-1, keepdims=True)
    inv = lax.rsqrt(ssq * (1.0 / XATTN_WIDTH) + EPS)
    for hh in range(XATTN_HEADS):
        sl = slice(hh * XATTN_HEAD_DIM, (hh + 1) * XATTN_HEAD_DIM)
        xo_out[:, sl] = (outs[hh] * inv * gxo_ref[:, sl]).astype(BF16)


def _inproj(x2d, seq, g_norm, w_in, g_q2, g_k2, cos2, sin2, xk, xv, g_xq, g_xo):
    rows, d = x2d.shape
    tm = ROW_TILE
    tn = DIFF_WIDTH
    n_in = w_in.shape[1]
    tiles_per_seq = seq // tm
    mem_len = xk.shape[1]
    blk = (tm * d * 4 + 2 * tm * LANES * 4 + 2 * mem_len * XATTN_WIDTH * 2
           + 3 * tm * tn * 2 + tm * SSM_WIDTH * 2 + tm * XATTN_WIDTH * 2)
    row = lambda i: (i, 0)
    const = lambda i: (0, 0)
    return pl.pallas_call(
        _inproj_kernel,
        grid=(rows // tm,),
        in_specs=[
            pl.BlockSpec((tm, d), row),
            pl.BlockSpec((1, d), const),
            pl.BlockSpec((d, n_in), const, pipeline_mode=pl.Buffered(1)),
            pl.BlockSpec((1, LANES), const),
            pl.BlockSpec((1, LANES), const),
            pl.BlockSpec((tm, LANES), lambda i: (i % tiles_per_seq, 0)),
            pl.BlockSpec((tm, LANES), lambda i: (i % tiles_per_seq, 0)),
            pl.BlockSpec((1, mem_len, XATTN_WIDTH), lambda i: (i // tiles_per_seq, 0, 0)),
            pl.BlockSpec((1, mem_len, XATTN_WIDTH), lambda i: (i // tiles_per_seq, 0, 0)),
            pl.BlockSpec((1, XATTN_HEAD_DIM), const),
            pl.BlockSpec((1, XATTN_WIDTH), const),
        ],
        out_specs=[
            pl.BlockSpec((tm, tn), row),
            pl.BlockSpec((tm, tn), row),
            pl.BlockSpec((1, tn, tm), lambda i: (i // tiles_per_seq, 0, i % tiles_per_seq)),
            pl.BlockSpec((SSM_TILES, tm // SSM_CHUNK, SSM_CHUNK * LANES), lambda i: (0, i, 0)),
            pl.BlockSpec((tm, XATTN_WIDTH), row),
        ],
        out_shape=[
            jax.ShapeDtypeStruct((rows, tn), BF16),
            jax.ShapeDtypeStruct((rows, tn), BF16),
            jax.ShapeDtypeStruct((rows // seq, tn, seq), BF16),
            jax.ShapeDtypeStruct((SSM_TILES, rows // SSM_CHUNK, SSM_CHUNK * LANES), BF16),
            jax.ShapeDtypeStruct((rows, XATTN_WIDTH), BF16),
        ],
        scratch_shapes=[pltpu.VMEM((SSM_TILES, tm, LANES), F32)],
        compiler_params=pltpu.CompilerParams(
            dimension_semantics=("parallel",),
            vmem_limit_bytes=_vmem_limit(blk, d * n_in * 2 + tm * d * 2 + 6 * tm * tn * 4)),
        name="inproj",
    )(x2d, g_norm, w_in, g_q2, g_k2, cos2, sin2, xk, xv, g_xq, g_xo)


def _attn_kernel(lq1_ref, lk1_ref, lq2_ref, lk2_ref, gsub_ref, q_ref, k_ref, vt_ref, o_ref,
                 s_scr, acc_scr, *, lambda_init):
    qi = pl.program_id(2)
    tq = q_ref.shape[1]
    tk = ATTN_KEY_TILE
    blocks_per_q = tq // tk
    q_t = q_ref[0].astype(F32).T.astype(BF16)
    feat = lax.broadcasted_iota(jnp.int32, (LANES, tq), 0)
    zero = jnp.zeros_like(q_t)
    qs = (jnp.where(feat < DIFF_QK_DIM, q_t, zero), jnp.where(feat >= DIFF_QK_DIM, q_t, zero))

    acc_scr[...] = jnp.zeros(acc_scr.shape, F32)

    def scores(kb, slot):
        k = k_ref[0, pl.ds(pl.multiple_of(kb * tk, tk), tk), :]
        for c in range(2):
            s_scr[slot, c] = jnp.dot(k, qs[c], preferred_element_type=F32)

    def softmax_pv(kb, slot, stats, diag_index=None):
        vt = vt_ref[0, :, pl.ds(pl.multiple_of(kb * tk, tk), tk)]
        vt_ones = jnp.concatenate([vt, jnp.ones((ATTN_ONES_ROWS, tk), BF16)], axis=0)
        new_stats = []
        for c in range(2):
            m_old = stats[c]
            s = s_scr[slot, c]
            if diag_index is not None:
                key = lax.broadcasted_iota(jnp.int32, (tk, tq), 0) + diag_index * tk
                qry = lax.broadcasted_iota(jnp.int32, (tk, tq), 1)
                s = jnp.where(key <= qry, s, -jnp.inf)
            m_new = jnp.maximum(m_old, jnp.max(s, axis=0, keepdims=True))
            alpha = jnp.exp2(m_old - m_new)
            p = jnp.exp2((s - m_new).astype(BF16))
            acc_scr[c] = alpha * acc_scr[c] + jnp.dot(vt_ones, p, preferred_element_type=F32)
            new_stats.append(m_new)
        return tuple(new_stats)

    def pair(i, stats):
        kb = blocks_per_q * i
        for j in range(blocks_per_q):
            scores(kb + j + 1, (j + 1) % 2)
            stats = softmax_pv(kb + j, j % 2, stats)
        return stats

    init = (jnp.full((1, tq), -jnp.inf, F32),) * 2
    def double_pair(i2, stats):
        return pair(2 * i2 + 1, pair(2 * i2, stats))

    scores(0, 0)
    stats = lax.fori_loop(0, qi // 2, double_pair, init)
    stats = lax.cond(qi % 2 == 1, lambda st: pair(qi - 1, st), lambda st: st, stats)
    kb = blocks_per_q * qi
    for j in range(blocks_per_q):
        if j + 1 < blocks_per_q:
            scores(kb + j + 1, (j + 1) % 2)
        stats = softmax_pv(kb + j, j % 2, stats, diag_index=j)

    lam = (jnp.exp(jnp.sum(lq1_ref[...] * lk1_ref[...], axis=-1, keepdims=True))
           - jnp.exp(jnp.sum(lq2_ref[...] * lk2_ref[...], axis=-1, keepdims=True))
           + lambda_init)
    nv = DIFF_V_DIM
    o_t = (acc_scr[0, 0:nv] / acc_scr[0, nv:nv + 1]
           - lam * (acc_scr[1, 0:nv] / acc_scr[1, nv:nv + 1]))
    ms = jnp.mean(o_t * o_t, axis=0, keepdims=True)
    o_t = o_t * lax.rsqrt(ms + EPS) * (gsub_ref[...] * (1.0 - lambda_init))
    o_ref[0] = o_t.T.astype(BF16)


def _attention(q, k, vt, lq1, lk1, lq2, lk2, g_subln_col, lambda_init):
    b, s, _ = q.shape
    tq = ATTN_TILE
    tk = ATTN_KEY_TILE
    assert tq % (2 * tk) == 0
    vec = pl.BlockSpec((1, DIFF_QK_DIM), lambda bi, h, qi: (0, 0))
    blk = 2 * tq * LANES * 2 + 2 * s * LANES * 2
    return pl.pallas_call(
        functools.partial(_attn_kernel, lambda_init=lambda_init),
        grid=(b, DIFF_HEADS, s // tq),
        in_specs=[
            vec, vec, vec, vec,
            pl.BlockSpec((DIFF_V_DIM, 1), lambda bi, h, qi: (0, 0)),
            pl.BlockSpec((1, tq, LANES), lambda bi, h, qi: (bi, qi, h)),
            pl.BlockSpec((1, s, LANES), lambda bi, h, qi: (bi, 0, h)),
            pl.BlockSpec((1, DIFF_V_DIM, s), lambda bi, h, qi: (bi, h, 0)),
        ],
        out_specs=pl.BlockSpec((1, tq, LANES), lambda bi, h, qi: (bi, qi, h)),
        out_shape=jax.ShapeDtypeStruct((b, s, DIFF_WIDTH), BF16),
        scratch_shapes=[
            pltpu.VMEM((2, 2, tk, tq), F32),
            pltpu.VMEM((2, DIFF_V_DIM + ATTN_ONES_ROWS, tq), F32),
        ],
        compiler_params=pltpu.CompilerParams(
            dimension_semantics=("parallel", "parallel", "arbitrary"),
            vmem_limit_bytes=_vmem_limit(blk, 2 * DIFF_V_DIM * tq * 4 + 12 * tk * tq * 4)),
        name="diffattn",
    )(lq1, lk1, lq2, lk2, g_subln_col, q, k, vt)


def _ssm_prep_kernel(lr_row, li_row, ld_row, lr_col, li_col, ld_col,
                     bre_ref, bim_ref, cre_ref, cim_ref, d_ref,
                     m_out, w_out, n_out, a_out):
    t_len = SSM_CHUNK
    ns = SSM_TILE_STATE

    def lam_bar(lr, li, ld):
        dt = jnp.exp(ld)
        e = jnp.exp(lr * dt)
        return e * jnp.cos(li * dt), e * jnp.sin(li * dt)

    lam_r = lr_row[0]
    lam_i = li_row[0]
    br, bi = lam_bar(lam_r, lam_i, ld_row[0])
    den = lam_r * lam_r + lam_i * lam_i
    nr = br - 1.0
    coef_r = (nr * lam_r + bi * lam_i) / den
    coef_i = (bi * lam_r - nr * lam_i) / den
    b_r = bre_ref[0]
    b_i = bim_ref[0]
    bb_r = coef_r * b_r - coef_i * b_i
    bb_i = coef_r * b_i + coef_i * b_r
    pr = jnp.ones_like(br)
    pi = jnp.zeros_like(br)
    for tau in range(t_len):
        rows = slice((t_len - 1 - tau) * LANES, (t_len - tau) * LANES)
        w_out[0, rows, 0:ns] = (bb_r * pr - bb_i * pi).astype(BF16)
        w_out[0, rows, ns:2 * ns] = (bb_r * pi + bb_i * pr).astype(BF16)
        pr, pi = pr * br - pi * bi, pr * bi + pi * br
    a_out[0, :, 0:ns] = pr
    a_out[0, :, ns:2 * ns] = pi

    cr, ci = lam_bar(lr_col[0], li_col[0], ld_col[0])
    c_r = cre_ref[0]
    c_i = cim_ref[0]
    qr = jnp.ones_like(cr)
    qi = jnp.zeros_like(cr)
    eye = (lax.broadcasted_iota(jnp.int32, (LANES, LANES), 0)
           == lax.broadcasted_iota(jnp.int32, (LANES, LANES), 1))
    kblk = []
    for tau in range(t_len + 1):
        cl_r = c_r * qr - c_i * qi
        cl_i = c_r * qi + c_i * qr
        if tau < t_len:
            kb = (jnp.dot(bb_r.astype(BF16), cl_r.astype(BF16), preferred_element_type=F32)
                  - jnp.dot(bb_i.astype(BF16), cl_i.astype(BF16), preferred_element_type=F32))
            if tau == 0:
                kb = kb + jnp.where(eye, d_ref[0], 0.0)
            kblk.append(kb.astype(BF16))
        if tau >= 1:
            cols = slice((tau - 1) * LANES, tau * LANES)
            n_out[0, 0:ns, cols] = cl_r.astype(BF16)
            n_out[0, ns:2 * ns, cols] = (-cl_i).astype(BF16)
        qr, qi = qr * cr - qi * ci, qr * ci + qi * cr
    zeros = jnp.zeros((LANES, LANES), BF16)
    for s in range(t_len):
        for t in range(t_len):
            m_out[0, s * LANES:(s + 1) * LANES, t * LANES:(t + 1) * LANES] = (
                kblk[t - s] if t >= s else zeros)


def _ssm_prep(lam_re, lam_im, log_dt, b_re, b_im, c_re, c_im, d_skip):
    nj, gj, p, h = SSM_TILES, SSM_LANE_GROUPS, SSM_STATE, SSM_GROUP_CH
    ns, tl = SSM_TILE_STATE, SSM_CHUNK * LANES
    eye = jnp.eye(gj, dtype=F32)
    dt_full = jnp.broadcast_to(log_dt[:, None], (SSM_GROUPS, p))

    def expand_b(b):
        b4 = b.reshape(nj, gj, p, h).transpose(0, 1, 3, 2)
        return (b4[:, :, :, None, :] * eye[None, :, None, :, None]).reshape(nj, LANES, ns)

    def expand_c(c):
        c4 = c.reshape(nj, gj, h, p).transpose(0, 1, 3, 2)
        return (c4[:, :, :, None, :] * eye[None, :, None, :, None]).reshape(nj, ns, LANES)

    args = (lam_re.reshape(nj, 1, ns), lam_im.reshape(nj, 1, ns), dt_full.reshape(nj, 1, ns),
            lam_re.reshape(nj, ns, 1), lam_im.reshape(nj, ns, 1), dt_full.reshape(nj, ns, 1),
            expand_b(b_re), expand_b(b_im), expand_c(c_re), expand_c(c_im),
            d_skip.reshape(nj, 1, LANES))
    spec = lambda shp: pl.BlockSpec((1,) + shp, lambda j: (j, 0, 0))
    out_bytes = tl * tl * 2 + 2 * tl * 2 * ns * 2 + 2 * ns * 4
    return pl.pallas_call(
        _ssm_prep_kernel,
        grid=(nj,),
        in_specs=[spec((1, ns))] * 3 + [spec((ns, 1))] * 3 + [spec((LANES, ns))] * 2
                 + [spec((ns, LANES))] * 2 + [spec((1, LANES))],
        out_specs=[spec((tl, tl)), spec((tl, 2 * ns)), spec((2 * ns, tl)), spec((1, 2 * ns))],
        out_shape=[
            jax.ShapeDtypeStruct((nj, tl, tl), BF16),
            jax.ShapeDtypeStruct((nj, tl, 2 * ns), BF16),
            jax.ShapeDtypeStruct((nj, 2 * ns, tl), BF16),
            jax.ShapeDtypeStruct((nj, 1, 2 * ns), F32),
        ],
        compiler_params=pltpu.CompilerParams(
            dimension_semantics=("parallel",),
            vmem_limit_bytes=_vmem_limit(out_bytes + 7 * LANES * ns * 4, 8 << 20)),
        name="ssm_prep",
    )(*args)


def _ssm_sum_kernel(u_ref, w_ref, s_out):
    s_out[0] = jnp.dot(u_ref[0], w_ref[0], preferred_element_type=F32)


def _ssm_sum(u2, w_op):
    nj, rows, tl = u2.shape
    ns2 = w_op.shape[2]
    spec = lambda shp: pl.BlockSpec((1,) + shp, lambda j: (j, 0, 0))
    blk = rows * tl * 2 + tl * ns2 * 2 + rows * ns2 * 4
    return pl.pallas_call(
        _ssm_sum_kernel,
        grid=(nj,),
        in_specs=[spec((rows, tl)), spec((tl, ns2))],
        out_specs=spec((rows, ns2)),
        out_shape=jax.ShapeDtypeStruct((nj, rows, ns2), F32),
        compiler_params=pltpu.CompilerParams(
            dimension_semantics=("parallel",), vmem_limit_bytes=_vmem_limit(blk)),
        name="ssm_sum",
    )(u2, w_op)


def _ssm_scan_kernel(s_ref, a_ref, x_out, *, n_batch, n_chunk):
    ns = SSM_TILE_STATE
    a = a_ref[0]
    ar = a[:, 0:ns]
    ai = a[:, ns:2 * ns]

    def body(c, carry):
        new = []
        for b in range(n_batch):
            xr, xi = carry[2 * b], carry[2 * b + 1]
            r = b * n_chunk + c
            x_out[0, pl.ds(r, 1), 0:ns] = xr
            x_out[0, pl.ds(r, 1), ns:2 * ns] = xi
            s = s_ref[0, pl.ds(r, 1), :]
            new.append(ar * xr - ai * xi + s[:, 0:ns])
            new.append(ar * xi + ai * xr + s[:, ns:2 * ns])
        return tuple(new)

    init = tuple(jnp.zeros((1, ns), F32) for _ in range(2 * n_batch))
    lax.fori_loop(0, n_chunk, body, init)


def _ssm_scan(s_sum, a_pow, n_batch):
    nj, rows, ns2 = s_sum.shape
    spec = lambda shp: pl.BlockSpec((1,) + shp, lambda j: (j, 0, 0))
    return pl.pallas_call(
        functools.partial(_ssm_scan_kernel, n_batch=n_batch, n_chunk=rows // n_batch),
        grid=(nj,),
        in_specs=[spec((rows, ns2)), spec((1, ns2))],
        out_specs=spec((rows, ns2)),
        out_shape=jax.ShapeDtypeStruct((nj, rows, ns2), F32),
        compiler_params=pltpu.CompilerParams(
            dimension_semantics=("parallel",),
            vmem_limit_bytes=_vmem_limit(2 * rows * ns2 * 4)),
        name="ssm_scan",
    )(s_sum, a_pow)


def _gelu_tanh(x):
    return x * (0.5 * (1.0 + jnp.tanh(math.sqrt(2.0 / math.pi) * (x + 0.044715 * (x * x * x)))))


def _ssm_out_kernel(u_ref, m_ref, x_ref, n_ref, z_out):
    xb = x_ref[0].astype(BF16)
    wide = 2 * LANES
    for n in range(SSM_CHUNK // 2):
        cols = slice(n * wide, (n + 1) * wide)
        depth = (n + 1) * wide
        y = jnp.dot(u_ref[0, :, 0:depth], m_ref[0, 0:depth, cols], preferred_element_type=F32)
        y = y + jnp.dot(xb, n_ref[0, :, cols], preferred_element_type=F32)
        z_out[0, :, cols] = _gelu_tanh(y).astype(BF16)


def _ssm_out(u2, m_op, x_state, n_op):
    nj, rows, tl = u2.shape
    ns2 = x_state.shape[2]
    n_split = 2
    tr = rows // n_split
    rspec = lambda shp: pl.BlockSpec((1,) + shp, lambda j, r: (j, r, 0))
    cspec = lambda shp: pl.BlockSpec((1,) + shp, lambda j, r: (j, 0, 0))
    blk = tr * tl * 2 + tl * tl * 2 + tr * ns2 * 4 + ns2 * tl * 2 + tr * tl * 2
    return pl.pallas_call(
        _ssm_out_kernel,
        grid=(nj, n_split),
        in_specs=[rspec((tr, tl)), cspec((tl, tl)), rspec((tr, ns2)), cspec((ns2, tl))],
        out_specs=rspec((tr, tl)),
        out_shape=jax.ShapeDtypeStruct((nj, rows, tl), BF16),
        compiler_params=pltpu.CompilerParams(
            dimension_semantics=("parallel", "arbitrary"),
            vmem_limit_bytes=_vmem_limit(blk, 3 * tr * tl * 4)),
        name="ssm_out",
    )(u2, m_op, x_state, n_op)


def _outproj_kernel(a_ref, z_ref, xo_ref, x_ref, wglu_ref, bglu_ref, gs_ref, wo_ref, gm_ref,
                    h_out, hm_out, z_scr):
    tm = x_ref.shape[0]
    half = tm // 2
    halves = [slice(r * half, (r + 1) * half) for r in range(2)]
    for c in range(SSM_TILES):
        for t in range(SSM_CHUNK):
            z_scr[c, pl.ds(t, tm // SSM_CHUNK, stride=SSM_CHUNK), :] = (
                z_ref[c, :, t * LANES:(t + 1) * LANES].astype(F32))
    zs = [jnp.concatenate([z_scr[c, rows, :] for c in range(SSM_TILES)], axis=-1).astype(BF16)
          for rows in halves]
    gates = [jnp.dot(z, wglu_ref[...], preferred_element_type=F32) + bglu_ref[...] for z in zs]
    for rows, z, gate_in in zip(halves, zs, gates):
        acc = jnp.dot(a_ref[rows, :], wo_ref[0:DIFF_WIDTH, :], preferred_element_type=F32)
        acc = acc + jnp.dot(xo_ref[rows, :], wo_ref[DIFF_WIDTH + SSM_WIDTH:, :],
                            preferred_element_type=F32)
        zg = z.astype(F32) * (1.0 / (1.0 + jnp.exp(-gate_in)))
        s_out = _rms(zg, gs_ref[...]).astype(BF16)
        acc = acc + jnp.dot(s_out, wo_ref[DIFF_WIDTH:DIFF_WIDTH + SSM_WIDTH, :],
                            preferred_element_type=F32)
        h = x_ref[rows, :] + acc
        h_out[rows, :] = h
        hm_out[rows, :] = _rms(h, gm_ref[...]).astype(BF16)


def _outproj(a_out, z, xo, x2d, w_glu, b_glu, g_ssm, w_out, g_mlp):
    rows, d = x2d.shape
    tm = ROW_TILE
    mix = w_out.shape[0]
    row = lambda i: (i, 0)
    const = lambda i: (0, 0)
    blk = (tm * DIFF_WIDTH * 2 + tm * SSM_WIDTH * 2 + tm * XATTN_WIDTH * 2 + tm * d * 4
           + SSM_WIDTH * SSM_WIDTH * 2 + mix * d * 2 + tm * d * 4 + tm * d * 2)
    return pl.pallas_call(
        _outproj_kernel,
        grid=(rows // tm,),
        in_specs=[
            pl.BlockSpec((tm, DIFF_WIDTH), row),
            pl.BlockSpec((SSM_TILES, tm // SSM_CHUNK, SSM_CHUNK * LANES), lambda i: (0, i, 0)),
            pl.BlockSpec((tm, XATTN_WIDTH), row),
            pl.BlockSpec((tm, d), row),
            pl.BlockSpec((SSM_WIDTH, SSM_WIDTH), const),
            pl.BlockSpec((1, SSM_WIDTH), const),
            pl.BlockSpec((1, SSM_WIDTH), const),
            pl.BlockSpec((mix, d), const),
            pl.BlockSpec((1, d), const),
        ],
        out_specs=[pl.BlockSpec((tm, d), row), pl.BlockSpec((tm, d), row)],
        out_shape=[jax.ShapeDtypeStruct((rows, d), F32), jax.ShapeDtypeStruct((rows, d), BF16)],
        scratch_shapes=[pltpu.VMEM((SSM_TILES, tm, LANES), F32)],
        compiler_params=pltpu.CompilerParams(
            dimension_semantics=("parallel",),
            vmem_limit_bytes=_vmem_limit(blk, 3 * tm * d * 4)),
        name="outproj",
    )(a_out, z, xo, x2d, w_glu, b_glu, g_ssm, w_out, g_mlp)


def _mlp_kernel(hm_ref, w1_ref, w2_ref, h_ref, o_ref):
    f = pl.program_id(1)

    @pl.when(f == 0)
    def _():
        o_ref[...] = h_ref[...]

    ff = jnp.dot(hm_ref[...], w1_ref[...], preferred_element_type=F32)
    ff = jnp.square(jnp.maximum(ff, 0.0)).astype(BF16)
    o_ref[...] += jnp.dot(ff, w2_ref[...], preferred_element_type=F32)


def _mlp(hm, h, w1, w2):
    rows, d = h.shape
    d_ff = w1.shape[1]
    tm, tf = ROW_TILE, MLP_FF_TILE
    blk = tm * d * 2 + d * tf * 2 + tf * d * 2 + tm * d * 4 + tm * d * 4
    return pl.pallas_call(
        _mlp_kernel,
        grid=(rows // tm, d_ff // tf),
        in_specs=[
            pl.BlockSpec((tm, d), lambda i, f: (i, 0)),
            pl.BlockSpec((d, tf), lambda i, f: (0, f)),
            pl.BlockSpec((tf, d), lambda i, f: (f, 0)),
            pl.BlockSpec((tm, d), lambda i, f: (i, 0)),
        ],
        out_specs=pl.BlockSpec((tm, d), lambda i, f: (i, 0)),
        out_shape=jax.ShapeDtypeStruct((rows, d), F32),
        compiler_params=pltpu.CompilerParams(
            dimension_semantics=("parallel", "arbitrary"),
            vmem_limit_bytes=_vmem_limit(blk, tm * tf * 6 + tm * d * 4)),
        name="mlp",
    )(hm, w1, w2, h)


def _rotary_tables(seq_len):
    dim = DIFF_QK_DIM
    inv_freq = 1.0 / (ROPE_THETA ** (jnp.arange(0, dim, 2, dtype=F32) / dim))
    freqs = jnp.arange(seq_len, dtype=F32)[:, None] * inv_freq[None, :]
    emb = jnp.concatenate([freqs, freqs, freqs, freqs], axis=-1)
    return jnp.cos(emb), jnp.sin(emb)


def kernel(x, mem, g_attn_norm, w_in, g_q, g_k, lam_q1, lam_k1, lam_q2, lam_k2, g_subln, lam_re, lam_im, log_dt, b_re, b_im, c_re, c_im, d_skip, w_glu, b_glu, g_ssm_out, g_mem, w_mem_kv, g_xq, g_xk, g_xattn_out, w_out, g_mlp_norm, w_mlp_in, w_mlp_out):
    b, s, d = x.shape
    m = mem.shape[1]
    depth = w_in.shape[0]
    rows = b * s
    cos2, sin2 = _rotary_tables(s)
    h = x.reshape(rows, d)
    mem2d = mem.reshape(b * m, d)
    row2 = lambda v: v.reshape(1, -1)
    for layer in range(depth):
        lambda_init = 0.8 - 0.6 * math.exp(-0.3 * layer)

        xk, xv = _memkv(mem2d, row2(g_mem[layer]), w_mem_kv[layer].astype(BF16),
                        row2(g_xk[layer]))
        q, k, vt, u, x_out = _inproj(
            h, s, row2(g_attn_norm[layer]), w_in[layer].astype(BF16),
            row2(jnp.tile(g_q[layer], 2)), row2(jnp.tile(g_k[layer], 2)), cos2, sin2,
            xk.reshape(b, m, XATTN_WIDTH), xv.reshape(b, m, XATTN_WIDTH),
            row2(g_xq[layer]), row2(g_xattn_out[layer]))

        a_out = _attention(q.reshape(b, s, DIFF_WIDTH), k.reshape(b, s, DIFF_WIDTH),
                           vt, row2(lam_q1[layer]), row2(lam_k1[layer]),
                           row2(lam_q2[layer]), row2(lam_k2[layer]),
                           g_subln[layer].reshape(DIFF_V_DIM, 1), lambda_init)

        m_op, w_op, n_op, a_pow = _ssm_prep(lam_re[layer], lam_im[layer], log_dt[layer],
                                            b_re[layer], b_im[layer], c_re[layer], c_im[layer],
                                            d_skip[layer])
        x_state = _ssm_scan(_ssm_sum(u, w_op), a_pow, b)
        z = _ssm_out(u, m_op, x_state, n_op)

        h, hm = _outproj(a_out.reshape(rows, DIFF_WIDTH), z, x_out, h,
                         w_glu[layer].astype(BF16), row2(b_glu[layer]), row2(g_ssm_out[layer]),
                         w_out[layer].astype(BF16), row2(g_mlp_norm[layer]))
        h = _mlp(hm, h, w_mlp_in[layer].astype(BF16), w_mlp_out[layer].astype(BF16))
    return h.reshape(b, s, d)
```
